```python
import jax, jax.numpy as jnp
from jax import lax
import numpy as np

D_MODEL = 1024
BATCH = 8
SEQ = 4096
DEPTH = 1
DEC_BATCH = 8
DEC_SEQ = 32
PAST_LEN = 4096

CHUNK = 64
EPS = 1e-6
GLA_HEADS = 4
GLA_DK = 128
GLA_DV = 256
GLA_RANK = 16
GLA_TAU = 16.0
SWA_HEADS = 16
SWA_KV_HEADS = 2
SWA_HD = 64
SWA_GROUP = SWA_HEADS // SWA_KV_HEADS
WINDOW = 128
WIN_CHUNKS = WINDOW // CHUNK
D_FF = 2816

GLA_QK_W = GLA_HEADS * GLA_DK
GLA_V_W = GLA_HEADS * GLA_DV
SWA_Q_W = SWA_HEADS * SWA_HD
SWA_KV_W = SWA_KV_HEADS * SWA_HD
SPLITS = (GLA_QK_W, GLA_QK_W, GLA_V_W, GLA_V_W, GLA_RANK,
          SWA_Q_W, SWA_KV_W, SWA_KV_W, D_MODEL, D_MODEL)
IN_W = sum(SPLITS)

kernel_name = "hybrid_gla_swa_macaron_stream_step"


def rmsnorm(x, g):
    xf = x.astype(jnp.float32)
    y = xf * lax.rsqrt(jnp.mean(xf * xf, axis=-1, keepdims=True) + EPS)
    return (y * g.astype(jnp.float32)).astype(x.dtype)


def half_swiglu(x, norm_g, w_in, w_out):
    h = rmsnorm(x, norm_g)
    gate, up = jnp.split(h @ w_in, 2, axis=-1)
    return x + 0.5 * ((jax.nn.silu(gate) * up) @ w_out)


def in_projection(h, w_in):
    idx, off = [], 0
    for s in SPLITS[:-1]:
        off += s
        idx.append(off)
    return jnp.split(h @ w_in, idx, axis=-1)


def gla_features(q, k, v, glr, w_gate_up, b_gate):
    B, T, _ = q.shape
    def heads(a, d):
        return a.reshape(B, T, GLA_HEADS, d).transpose(0, 2, 1, 3)
    q = heads(q, GLA_DK) * (GLA_DK ** -0.5)
    k = heads(k, GLA_DK)
    v = heads(v, GLA_DV)
    logit = (glr @ w_gate_up + b_gate).astype(jnp.float32)
    log_alpha = heads(jax.nn.log_sigmoid(logit) / GLA_TAU, GLA_DK)
    return q, k, v, log_alpha


def gla_block(S, q, k, v, la):
    q, k, v = (a.astype(jnp.float32) for a in (q, k, v))
    L = q.shape[2]
    b = jnp.cumsum(la, axis=2)
    causal = jnp.arange(L)[:, None] >= jnp.arange(L)[None, :]
    diff = b[:, :, :, None, :] - b[:, :, None, :, :]
    decay = jnp.exp(jnp.where(causal[None, None, :, :, None], diff, -jnp.inf))
    A = jnp.einsum('bhtd,bhsd,bhtsd->bhts', q, k, decay)
    o = A @ v + jnp.einsum('bhtd,bhde->bhte', q * jnp.exp(b), S)
    bL = b[:, :, -1:, :]
    S_new = jnp.exp(bL[:, :, 0, :])[..., None] * S + \
        jnp.einsum('bhsd,bhse->bhde', k * jnp.exp(bL - b), v)
    return S_new, o


def gla_prompt(q, k, v, la):
    B, H, S, _ = q.shape
    nc = S // CHUNK
    def blocks(a):
        return a.reshape(B, H, nc, CHUNK, a.shape[-1]).transpose(2, 0, 1, 3, 4)
    S0 = jnp.zeros((B, H, GLA_DK, GLA_DV), jnp.float32)
    S_fin, o = lax.scan(lambda s, c: gla_block(s, *c), S0,
                        (blocks(q), blocks(k), blocks(v), blocks(la)))
    o = o.transpose(1, 2, 0, 3, 4).reshape(B, H, S, GLA_DV)
    return o, S_fin


def sink_softmax(s, sinks):
    sk = sinks.astype(jnp.float32).reshape(SWA_KV_HEADS, SWA_GROUP)[:, :, None, None]
    m = jnp.maximum(jnp.max(s, axis=-1, keepdims=True), sk)
    p = jnp.exp(s - m)
    return p / (jnp.sum(p, axis=-1, keepdims=True) + jnp.exp(sk - m))


def swa_prompt(q, k, v, sinks):
    B, S, _ = q.shape
    nc = S // CHUNK
    nk = (WIN_CHUNKS + 1) * CHUNK
    pad = WIN_CHUNKS * CHUNK
    q = q.reshape(B, nc, CHUNK, SWA_KV_HEADS, SWA_GROUP, SWA_HD)
    k = k.reshape(B, S, SWA_KV_HEADS, SWA_HD)
    v = v.reshape(B, S, SWA_KV_HEADS, SWA_HD)
    def band(a):
        ap = jnp.pad(a, ((0, 0), (pad, 0), (0, 0), (0, 0)))
        ap = ap.reshape(B, nc + WIN_CHUNKS, CHUNK, SWA_KV_HEADS, SWA_HD)
        return jnp.concatenate([ap[:, i:i + nc] for i in range(WIN_CHUNKS + 1)], axis=2)
    kb, vb = band(k), band(v)
    key_pos = (jnp.arange(nc) * CHUNK)[:, None] - pad + jnp.arange(nk)[None, :]
    valid = key_pos >= 0
    s = jnp.einsum('bcqkgd,bcjkd->bckgqj', q, kb).astype(jnp.float32) * (SWA_HD ** -0.5)
    s = jnp.where(valid[None, :, None, None, None, :], s, -jnp.inf)
    p = sink_softmax(s, sinks)
    o = jnp.einsum('bckgqj,bcjkd->bcqkgd', p.astype(vb.dtype), vb).reshape(B, S, SWA_Q_W)
    return o, k[:, -WINDOW:], v[:, -WINDOW:]


def swa_sample(q, k, v, sinks, cache_k, cache_v):
    B, T, _ = q.shape
    q = q.reshape(B, T, SWA_KV_HEADS, SWA_GROUP, SWA_HD)
    k_all = jnp.concatenate([cache_k.astype(k.dtype), k.reshape(B, T, SWA_KV_HEADS, SWA_HD)], axis=1)
    v_all = jnp.concatenate([cache_v.astype(v.dtype), v.reshape(B, T, SWA_KV_HEADS, SWA_HD)], axis=1)
    s = jnp.einsum('bqkgd,bjkd->bkgqj', q, k_all).astype(jnp.float32) * (SWA_HD ** -0.5)
    p = sink_softmax(s, sinks)
    o = jnp.einsum('bkgqj,bjkd->bqkgd', p.astype(v_all.dtype), v_all).reshape(B, T, SWA_Q_W)
    n = cache_k.shape[1]
    return o, k_all[:, -n:], v_all[:, -n:]


def token_mixing(x, norm_g, w_in, w_gate_up, b_gate, gla_norm_g, sinks,
                 w_br_gla, w_br_swa, w_out, state_gla=None, cache_k=None, cache_v=None):
    B, T, _ = x.shape
    h = rmsnorm(x, norm_g)
    gq, gk, gv, gr, glr, sq, sk, sv, gate_gla, gate_swa = in_projection(h, w_in)
    q, k, v, la = gla_features(gq, gk, gv, glr, w_gate_up, b_gate)
    if state_gla is None:
        o_gla, s_gla = gla_prompt(q, k, v, la)
    else:
        s_gla, o_gla = gla_block(state_gla.astype(jnp.float32), q, k, v, la)
    o_gla = rmsnorm(o_gla, gla_norm_g).astype(h.dtype)
    o_gla = o_gla.transpose(0, 2, 1, 3).reshape(B, T, GLA_V_W) * jax.nn.silu(gr)
    if cache_k is None:
        o_swa, k_keep, v_keep = swa_prompt(sq, sk, sv, sinks)
    else:
        o_swa, k_keep, v_keep = swa_sample(sq, sk, sv, sinks, cache_k, cache_v)
    merged = jax.nn.sigmoid(gate_gla) * (o_gla @ w_br_gla) + jax.nn.sigmoid(gate_swa) * (o_swa @ w_br_swa)
    return x + merged @ w_out, s_gla, k_keep, v_keep


def macaron_layer(x, ffn1, mix, ffn2, state_gla=None, cache_k=None, cache_v=None):
    x = half_swiglu(x, *ffn1)
    x, s_gla, k_keep, v_keep = token_mixing(x, *mix, state_gla=state_gla, cache_k=cache_k, cache_v=cache_v)
    x = half_swiglu(x, *ffn2)
    return x, s_gla, k_keep, v_keep


def setup_inputs(seed: int = 0) -> dict:
    key = jax.random.key(seed)
    ks = jax.random.split(key, 24)
    f32 = jnp.float32
    def nrm(k, shape, scale):
        return jax.random.normal(k, shape, f32) * scale
    L = DEPTH
    return {
        "x_prompt": nrm(ks[0], (BATCH, SEQ, D_MODEL), 1.0),
        "x_sample": nrm(ks[1], (DEC_BATCH, DEC_SEQ, D_MODEL), 1.0),
        "state_gla": nrm(ks[2], (L, DEC_BATCH, GLA_HEADS, GLA_DK, GLA_DV), 1.0),
        "cache_swa_k": nrm(ks[3], (L, DEC_BATCH, WINDOW, SWA_KV_HEADS, SWA_HD), 1.0),
        "cache_swa_v": nrm(ks[4], (L, DEC_BATCH, WINDOW, SWA_KV_HEADS, SWA_HD), 1.0),
        "norm_ffn1": 1.0 + nrm(ks[5], (L, D_MODEL), 0.02),
        "w_ffn1_in": nrm(ks[6], (L, D_MODEL, 2 * D_FF), D_MODEL ** -0.5),
        "w_ffn1_out": nrm(ks[7], (L, D_FF, D_MODEL), D_FF ** -0.5),
        "norm_mix": 1.0 + nrm(ks[8], (L, D_MODEL), 0.02),
        "w_in": nrm(ks[9], (L, D_MODEL, IN_W), D_MODEL ** -0.5),
        "w_gla_gate_up": nrm(ks[10], (L, GLA_RANK, GLA_QK_W), GLA_RANK ** -0.5),
        "b_gla_gate": nrm(ks[11], (L, GLA_QK_W), 0.1),
        "gla_norm": 1.0 + nrm(ks[12], (L, GLA_DV), 0.02),
        "swa_sinks": nrm(ks[13], (L, SWA_HEADS), 0.5),
        "w_branch_gla": nrm(ks[14], (L, GLA_V_W, D_MODEL), GLA_V_W ** -0.5),
        "w_branch_swa": nrm(ks[15], (L, SWA_Q_W, D_MODEL), SWA_Q_W ** -0.5),
        "w_out": nrm(ks[16], (L, D_MODEL, D_MODEL), D_MODEL ** -0.5),
        "norm_ffn2": 1.0 + nrm(ks[17], (L, D_MODEL), 0.02),
        "w_ffn2_in": nrm(ks[18], (L, D_MODEL, 2 * D_FF), D_MODEL ** -0.5),
        "w_ffn2_out": nrm(ks[19], (L, D_FF, D_MODEL), D_FF ** -0.5),
        "norm_final": 1.0 + nrm(ks[20], (D_MODEL,), 0.02),
    }


def reference(x_prompt, x_sample, state_gla, cache_swa_k, cache_swa_v,
              norm_ffn1, w_ffn1_in, w_ffn1_out, norm_mix, w_in, w_gla_gate_up, b_gla_gate,
              gla_norm, swa_sinks, w_branch_gla, w_branch_swa, w_out,
              norm_ffn2, w_ffn2_in, w_ffn2_out, norm_final):
    yp, ys = x_prompt, x_sample
    sg_p, sk_p, sv_p, sg_s, sk_s, sv_s = [], [], [], [], [], []
    for l in range(DEPTH):
        ffn1 = (norm_ffn1[l], w_ffn1_in[l], w_ffn1_out[l])
        mix = (norm_mix[l], w_in[l], w_gla_gate_up[l], b_gla_gate[l], gla_norm[l], swa_sinks[l],
               w_branch_gla[l], w_branch_swa[l], w_out[l])
        ffn2 = (norm_ffn2[l], w_ffn2_in[l], w_ffn2_out[l])
        yp, g_p, k_p, v_p = macaron_layer(yp, ffn1, mix, ffn2)
        ys, g_s, k_s, v_s = macaron_layer(ys, ffn1, mix, ffn2, state_gla[l], cache_swa_k[l], cache_swa_v[l])
        sg_p.append(g_p); sk_p.append(k_p); sv_p.append(v_p)
        sg_s.append(g_s); sk_s.append(k_s); sv_s.append(v_s)
    y_prompt = rmsnorm(yp, norm_final)
    y_sample = rmsnorm(ys, norm_final)
    return (y_prompt, y_sample,
            jnp.stack(sg_p), jnp.stack(sk_p), jnp.stack(sv_p),
            jnp.stack(sg_s), jnp.stack(sk_s), jnp.stack(sv_s))
```

```python
import functools

import jax
import jax.numpy as jnp
from jax import lax
from jax.experimental import pallas as pl
from jax.experimental.pallas import tpu as pltpu

F32 = jnp.float32
BF16 = jnp.bfloat16

D_MODEL = 1024
D_FF = 2816
CHUNK = 64
CHUNK_SHIFT = CHUNK.bit_length() - 1
EPS = 1e-6
GLA_HEADS = 4
GLA_DK = 128
GLA_DV = 256
GLA_RANK = 16
GLA_TAU = 16.0
SWA_HEADS = 16
SWA_KV_HEADS = 2
SWA_HD = 64
SWA_GROUP = SWA_HEADS // SWA_KV_HEADS
WINDOW = 128
GLA_QK_W = GLA_HEADS * GLA_DK
GLA_V_W = GLA_HEADS * GLA_DV
SWA_Q_W = SWA_HEADS * SWA_HD
SWA_KV_W = SWA_KV_HEADS * SWA_HD

LANES = 128
RANK_PAD = LANES
FF_CHUNK = 256
N_FF_CHUNKS = D_FF // FF_CHUNK
VMEM_LIMIT = 56 * 1024 * 1024

_C_GQ = 0
_C_GK = _C_GQ + GLA_QK_W
_C_GLR = _C_GK + GLA_QK_W
_C_GV = _C_GLR + RANK_PAD
_C_GR = _C_GV + GLA_V_W
_C_SQ = _C_GR + GLA_V_W
_C_SK = _C_SQ + SWA_Q_W
_C_SV = _C_SK + SWA_KV_W
_C_GG = _C_SV + SWA_KV_W
_C_GS = _C_GG + D_MODEL
IN_COLS = _C_GS + D_MODEL


def _rmsnorm(x, g):
    ms = jnp.mean(x * x, axis=-1, keepdims=True)
    return x * lax.rsqrt(ms + EPS) * g


def _dot(a, b):
    return jnp.dot(a, b, preferred_element_type=F32)


def _dot_nt(a, b):
    return lax.dot_general(a, b, (((1,), (1,)), ((), ())), preferred_element_type=F32)


def _dot_tn(a, b):
    return lax.dot_general(a, b, (((0,), (0,)), ((), ())), preferred_element_type=F32)


def _ffn_kernel(x_ref, g_ref, w1_ref, w2_ref, gf_ref, o_ref, h_ref, acc_ref, *, final_norm):
    x = x_ref[...]
    h_ref[...] = _rmsnorm(x, g_ref[...]).astype(BF16)
    acc_ref[...] = jnp.zeros_like(acc_ref)

    def body(c, carry):
        gu = _dot(h_ref[...], w1_ref[c])
        gate = gu[:, :FF_CHUNK]
        up = gu[:, FF_CHUNK:]
        act = (gate * jax.nn.sigmoid(gate) * up).astype(BF16)
        acc_ref[...] += _dot(act, w2_ref[c])
        return carry

    lax.fori_loop(0, N_FF_CHUNKS, body, 0)
    y = x + 0.5 * acc_ref[...]
    if final_norm:
        y = _rmsnorm(y, gf_ref[...])
    o_ref[...] = y


def _resident(shape):
    nd = len(shape)
    return pl.BlockSpec(shape, lambda *_: (0,) * nd, pipeline_mode=pl.Buffered(1))


def _ffn(x, g, w1, w2, gf, *, final_norm, tm):
    n = x.shape[0]
    return pl.pallas_call(
        functools.partial(_ffn_kernel, final_norm=final_norm),
        grid=(n // tm,),
        in_specs=[
            pl.BlockSpec((tm, D_MODEL), lambda i: (i, 0)),
            _resident((1, D_MODEL)),
            _resident(w1.shape),
            _resident(w2.shape),
            _resident((1, D_MODEL)),
        ],
        out_specs=pl.BlockSpec((tm, D_MODEL), lambda i: (i, 0)),
        out_shape=jax.ShapeDtypeStruct((n, D_MODEL), F32),
        scratch_shapes=[pltpu.VMEM((tm, D_MODEL), BF16), pltpu.VMEM((tm, D_MODEL), F32)],
        compiler_params=pltpu.CompilerParams(
            dimension_semantics=("arbitrary",), vmem_limit_bytes=VMEM_LIMIT),
    )(x, g, w1, w2, gf)


def _ref_rows(p, m, row):
    n_rows, width = p.shape
    if m == 1:
        return jnp.where((row & 1) == 0, p, pltpu.roll(p, 1, 0))
    if m == 2:
        r = row & 3
        return jnp.where(r == 0, pltpu.roll(p, n_rows - 1, 0),
                         jnp.where(r == 1, p,
                                   jnp.where(r == 2, pltpu.roll(p, 1, 0), pltpu.roll(p, 2, 0))))
    nb = n_rows // (2 * m)
    p3 = p.reshape(nb, 2 * m, width)
    r3 = jnp.broadcast_to(p3[:, m - 1:m, :], (nb, 2 * m, width))
    return r3.reshape(n_rows, width)


def _gla_block(q, k, v, la, s_ref, gnorm):
    n_rows = q.shape[0]
    row = lax.broadcasted_iota(jnp.int32, (n_rows, 1), 0)
    levels = []
    m = 1
    while m < n_rows:
        levels.append(m)
        m *= 2
    p = la
    zs = []
    for m in levels:
        upper = (row & m) != 0
        r = _ref_rows(p, m, row)
        e = jnp.exp(jnp.where(upper, p, r - p))
        zs.append((jnp.where(upper, q, k) * e).astype(BF16))
        p = p + jnp.where(upper, r, 0.0)
    b = p
    b_last = b[n_rows - 1:n_rows, :]
    q_in = (q * jnp.exp(b)).astype(BF16)
    k_out = (k * jnp.exp(b_last - b)).astype(BF16)
    s_decay = jnp.exp(b_last)
    qb = q.astype(BF16)
    kb = k.astype(BF16)
    vb = v.astype(BF16)

    ti = lax.broadcasted_iota(jnp.int32, (n_rows, n_rows), 0)
    si = lax.broadcasted_iota(jnp.int32, (n_rows, n_rows), 1)
    split = jnp.where(ti > si, ti ^ si, 0)
    split_level = 31 - lax.clz(split)
    eye = ti == si

    outs = []
    for h in range(GLA_HEADS):
        ks = slice(h * GLA_DK, (h + 1) * GLA_DK)
        vs = slice(h * GLA_DV, (h + 1) * GLA_DV)
        a = jnp.where(eye, _dot_nt(qb[:, ks], kb[:, ks]), 0.0)
        for li, z in enumerate(zs):
            a = jnp.where(split_level == li, _dot_nt(z[:, ks], z[:, ks]), a)
        s_old = s_ref[h]
        o = _dot(a.astype(BF16), vb[:, vs]) + _dot(q_in[:, ks], s_old.astype(BF16))
        dcol = jnp.transpose(jnp.broadcast_to(s_decay[:, ks], (8, GLA_DK)))[:, 0:1]
        s_ref[h] = dcol * s_old + _dot_tn(k_out[:, ks], vb[:, vs])
        outs.append(_rmsnorm(o, gnorm))
    return jnp.concatenate(outs, axis=1)


def _swa_block(q, kn, ks, vn, vs, mask, sink_ref):
    tq = q.shape[0]
    nk = kn.shape[0]
    low = lax.broadcasted_iota(jnp.int32, (nk, LANES), 1) < SWA_HD
    low_o = lax.broadcasted_iota(jnp.int32, (tq, LANES), 1) < SWA_HD
    pairs = SWA_GROUP // 2
    out_tiles = [None] * (SWA_HEADS // 2)
    for g in range(SWA_KV_HEADS):
        k_lo, k_hi = (kn, ks) if g == 0 else (ks, kn)
        v_lo, v_hi = (vn, vs) if g == 0 else (vs, vn)
        kbd = jnp.concatenate([jnp.where(low, k_lo, 0.0), jnp.where(low, 0.0, k_hi)], axis=0).astype(BF16)
        vbd = jnp.concatenate([jnp.where(low, v_lo, 0.0), jnp.where(low, 0.0, v_hi)], axis=0).astype(BF16)
        qs = jnp.concatenate(
            [q[:, (g * pairs + i) * LANES:(g * pairs + i + 1) * LANES] for i in range(pairs)], axis=0).astype(BF16)
        s = _dot_nt(qs, kbd)
        p_rows, inv_rows = [], []
        for i in range(pairs):
            p_half, inv_half = [], []
            for half in range(2):
                sk = sink_ref[g * SWA_GROUP + 2 * i + half]
                sh = jnp.where(mask, s[i * tq:(i + 1) * tq, half * nk:(half + 1) * nk], -jnp.inf)
                mx = jnp.maximum(jnp.max(sh, axis=-1, keepdims=True), sk)
                ph = jnp.exp(sh - mx)
                den = jnp.sum(ph, axis=-1, keepdims=True) + jnp.exp(sk - mx)
                p_half.append(ph.astype(BF16))
                inv_half.append(1.0 / den)
            p_rows.append(jnp.concatenate(p_half, axis=1))
            inv_rows.append(jnp.where(low_o, inv_half[0], inv_half[1]))
        o2 = _dot(jnp.concatenate(p_rows, axis=0), vbd)
        for i in range(pairs):
            out_tiles[g * pairs + i] = o2[i * tq:(i + 1) * tq, :] * inv_rows[i]
    return jnp.concatenate(out_tiles, axis=1)


def _mix_kernel(*refs, is_prompt, tile, gla_block):
    if is_prompt:
        (sink_ref, x_ref, g_ref, win_ref, wgu_ref, bg_ref, gn_ref, wbg_ref, wbs_ref, wout_ref,
         y_ref, s_ref, ck_ref, cv_ref, kx_ref, vx_ref) = refs
    else:
        (sink_ref, x_ref, g_ref, win_ref, wgu_ref, bg_ref, gn_ref, wbg_ref, wbs_ref, wout_ref,
         s0_ref, ck0_ref, cv0_ref, y_ref, s_ref, ck_ref, cv_ref, kx_ref, vx_ref) = refs
    step = pl.program_id(1)

    if is_prompt:
        @pl.when(step == 0)
        def _():
            s_ref[...] = jnp.zeros_like(s_ref)
            ck_ref[...] = jnp.zeros_like(ck_ref)
            cv_ref[...] = jnp.zeros_like(cv_ref)
    else:
        s_ref[...] = s0_ref[...]
        ck_ref[...] = ck0_ref[...]
        cv_ref[...] = cv0_ref[...]

    x = x_ref[...]
    h = _rmsnorm(x, g_ref[...]).astype(BF16)

    pa = _dot(h, win_ref[:, _C_GQ:_C_GV])
    gv = _dot(h, win_ref[:, _C_GV:_C_GR])
    logit = _dot(pa[:, _C_GLR:_C_GLR + RANK_PAD].astype(BF16), wgu_ref[...]) + bg_ref[...]
    log_alpha = (jnp.minimum(logit, 0.0) - jnp.log1p(jnp.exp(-jnp.abs(logit)))) * (1.0 / GLA_TAU)
    gq = pa[:, _C_GQ:_C_GQ + GLA_QK_W] * (GLA_DK ** -0.5)
    gk = pa[:, _C_GK:_C_GK + GLA_QK_W]
    gn = gn_ref[...]
    o_parts = []
    for c in range(tile // gla_block):
        rs = slice(c * gla_block, (c + 1) * gla_block)
        o_parts.append(_gla_block(gq[rs], gk[rs], gv[rs], log_alpha[rs], s_ref, gn))
    o_gla = o_parts[0] if len(o_parts) == 1 else jnp.concatenate(o_parts, axis=0)
    gr = _dot(h, win_ref[:, _C_GR:_C_SQ])
    o_gla = (o_gla * (gr * jax.nn.sigmoid(gr))).astype(BF16)

    pb = _dot(h, win_ref[:, _C_SQ:_C_GG])
    sq = pb[:, :SWA_Q_W] * (SWA_HD ** -0.5)
    sk = pb[:, SWA_Q_W:SWA_Q_W + SWA_KV_W]
    sv = pb[:, SWA_Q_W + SWA_KV_W:]
    kx_ref[0:WINDOW, :] = ck_ref[...]
    vx_ref[0:WINDOW, :] = cv_ref[...]
    kx_ref[WINDOW:WINDOW + tile, :] = sk
    vx_ref[WINDOW:WINDOW + tile, :] = sv
    n_ext = kx_ref.shape[0]
    if WINDOW + tile < n_ext:
        kx_ref[WINDOW + tile:, :] = jnp.zeros((n_ext - WINDOW - tile, LANES), F32)
        vx_ref[WINDOW + tile:, :] = jnp.zeros((n_ext - WINDOW - tile, LANES), F32)
    ck_ref[...] = kx_ref[tile:tile + WINDOW, :]
    cv_ref[...] = vx_ref[tile:tile + WINDOW, :]

    nk = 2 * WINDOW
    if is_prompt:
        tq = 2 * CHUNK
        qi = lax.broadcasted_iota(jnp.int32, (tq, nk), 0) >> CHUNK_SHIFT
        kj = lax.broadcasted_iota(jnp.int32, (tq, nk), 1)
        kc = kj >> CHUNK_SHIFT
        band = (kc >= qi) & (kc <= qi + WINDOW // CHUNK)
        o_rows = []
        for pg in range(tile // tq):
            kx = kx_ref[pg * tq:pg * tq + nk, :]
            vx = vx_ref[pg * tq:pg * tq + nk, :]
            if pg == 0:
                mask = band & ((kj >= WINDOW) | (step > 0))
            else:
                mask = band
            o_rows.append(_swa_block(sq[pg * tq:(pg + 1) * tq], kx, pltpu.roll(kx, SWA_HD, 1),
                                     vx, pltpu.roll(vx, SWA_HD, 1), mask, sink_ref))
        o_swa = jnp.concatenate(o_rows, axis=0)
    else:
        kj = lax.broadcasted_iota(jnp.int32, (tile, nk), 1)
        kx = kx_ref[...]
        vx = vx_ref[...]
        o_swa = _swa_block(sq, kx, pltpu.roll(kx, SWA_HD, 1), vx, pltpu.roll(vx, SWA_HD, 1),
                           kj < WINDOW + tile, sink_ref)
    o_swa = o_swa.astype(BF16)

    gg = _dot(h, win_ref[:, _C_GG:_C_GS])
    gs = _dot(h, win_ref[:, _C_GS:IN_COLS])
    merged = jax.nn.sigmoid(gg) * _dot(o_gla, wbg_ref[...]) + jax.nn.sigmoid(gs) * _dot(o_swa, wbs_ref[...])
    y_ref[...] = x + _dot(merged.astype(BF16), wout_ref[...])


def _mix(x, sinks, g, win, wgu, bg, gn, wbg, wbs, wout, state0=None, ck0=None, cv0=None, *, tile, gla_block):
    batch, seq, _ = x.shape
    is_prompt = state0 is None
    n_ext = WINDOW + tile if is_prompt else 2 * WINDOW
    in_specs = [
        pl.BlockSpec(memory_space=pltpu.SMEM),
        pl.BlockSpec((None, tile, D_MODEL), lambda b, i: (b, i, 0)),
        _resident((1, D_MODEL)),
        _resident(win.shape),
        _resident(wgu.shape),
        _resident((1, GLA_QK_W)),
        _resident((1, GLA_DV)),
        _resident(wbg.shape),
        _resident(wbs.shape),
        _resident(wout.shape),
    ]
    args = [sinks, x, g, win, wgu, bg, gn, wbg, wbs, wout]
    state_spec = pl.BlockSpec((None, GLA_HEADS, GLA_DK, GLA_DV), lambda b, i: (b, 0, 0, 0))
    cache_spec = pl.BlockSpec((None, WINDOW, SWA_KV_W), lambda b, i: (b, 0, 0))
    if not is_prompt:
        in_specs += [state_spec, cache_spec, cache_spec]
        args += [state0, ck0, cv0]
    return pl.pallas_call(
        functools.partial(_mix_kernel, is_prompt=is_prompt, tile=tile, gla_block=gla_block),
        grid=(batch, seq // tile),
        in_specs=in_specs,
        out_specs=[
            pl.BlockSpec((None, tile, D_MODEL), lambda b, i: (b, i, 0)),
            state_spec, cache_spec, cache_spec,
        ],
        out_shape=[
            jax.ShapeDtypeStruct((batch, seq, D_MODEL), F32),
            jax.ShapeDtypeStruct((batch, GLA_HEADS, GLA_DK, GLA_DV), F32),
            jax.ShapeDtypeStruct((batch, WINDOW, SWA_KV_W), F32),
            jax.ShapeDtypeStruct((batch, WINDOW, SWA_KV_W), F32),
        ],
        scratch_shapes=[pltpu.VMEM((n_ext, LANES), F32), pltpu.VMEM((n_ext, LANES), F32)],
        compiler_params=pltpu.CompilerParams(
            dimension_semantics=("arbitrary", "arbitrary"), vmem_limit_bytes=VMEM_LIMIT),
    )(*args)


def _ffn_weights(norm_g, w_in, w_out):
    wg = w_in[:, :D_FF].reshape(D_MODEL, N_FF_CHUNKS, FF_CHUNK)
    wu = w_in[:, D_FF:].reshape(D_MODEL, N_FF_CHUNKS, FF_CHUNK)
    w1 = jnp.concatenate([wg, wu], axis=2).transpose(1, 0, 2).astype(BF16)
    w2 = w_out.reshape(N_FF_CHUNKS, FF_CHUNK, D_MODEL).astype(BF16)
    return norm_g.reshape(1, D_MODEL), w1, w2


def _mix_weights(w_in, w_gate_up):
    o = 0
    parts = {}
    for name, width in (("gq", GLA_QK_W), ("gk", GLA_QK_W), ("gv", GLA_V_W), ("gr", GLA_V_W),
                        ("glr", GLA_RANK), ("sq", SWA_Q_W), ("sk", SWA_KV_W), ("sv", SWA_KV_W),
                        ("gg", D_MODEL), ("gs", D_MODEL)):
        parts[name] = w_in[:, o:o + width]
        o += width
    glr = jnp.pad(parts["glr"], ((0, 0), (0, RANK_PAD - GLA_RANK)))
    win = jnp.concatenate([parts["gq"], parts["gk"], glr, parts["gv"], parts["gr"], parts["sq"],
                           parts["sk"], parts["sv"], parts["gg"], parts["gs"]], axis=1).astype(BF16)
    wgu = jnp.pad(w_gate_up, ((0, RANK_PAD - GLA_RANK), (0, 0))).astype(BF16)
    return win, wgu


def _layer(x, ffn1, mixw, ffn2, gf, state0=None, ck0=None, cv0=None, *, tm, tile, gla_block):
    batch, seq, _ = x.shape
    x1 = _ffn(x.reshape(batch * seq, D_MODEL), *ffn1, gf, final_norm=False, tm=tm)
    x2, state, ck, cv = _mix(x1.reshape(batch, seq, D_MODEL), *mixw, state0, ck0, cv0,
                             tile=tile, gla_block=gla_block)
    y = _ffn(x2.reshape(batch * seq, D_MODEL), *ffn2, gf, final_norm=True, tm=tm)
    return (y.reshape(batch, seq, D_MODEL), state[None],
            ck.reshape(1, batch, WINDOW, SWA_KV_HEADS, SWA_HD),
            cv.reshape(1, batch, WINDOW, SWA_KV_HEADS, SWA_HD))


def kernel(x_prompt, x_sample, state_gla, cache_swa_k, cache_swa_v, norm_ffn1, w_ffn1_in, w_ffn1_out, norm_mix, w_in, w_gla_gate_up, b_gla_gate, gla_norm, swa_sinks, w_branch_gla, w_branch_swa, w_out, norm_ffn2, w_ffn2_in, w_ffn2_out, norm_final):
    ffn1 = _ffn_weights(norm_ffn1[0], w_ffn1_in[0], w_ffn1_out[0])
    ffn2 = _ffn_weights(norm_ffn2[0], w_ffn2_in[0], w_ffn2_out[0])
    win, wgu = _mix_weights(w_in[0], w_gla_gate_up[0])
    mixw = (swa_sinks[0], norm_mix[0].reshape(1, D_MODEL), win, wgu, b_gla_gate[0].reshape(1, GLA_QK_W),
            gla_norm[0].reshape(1, GLA_DV), w_branch_gla[0].astype(BF16), w_branch_swa[0].astype(BF16),
            w_out[0].astype(BF16))
    gf = norm_final.reshape(1, D_MODEL)
    dec_batch, dec_seq, _ = x_sample.shape
    yp, sg_p, sk_p, sv_p = _layer(x_prompt, ffn1, mixw, ffn2, gf, tm=512, tile=256, gla_block=128)
    ys, sg_s, sk_s, sv_s = _layer(
        x_sample, ffn1, mixw, ffn2, gf, state_gla[0],
        cache_swa_k[0].reshape(dec_batch, WINDOW, SWA_KV_W), cache_swa_v[0].reshape(dec_batch, WINDOW, SWA_KV_W),
        tm=dec_batch * dec_seq, tile=dec_seq, gla_block=dec_seq)
    return yp, ys, sg_p, sk_p, sv_p, sg_s, sk_s, sv_s
```

```python
import functools

import jax
import jax.numpy as jnp
from jax import lax
from jax.experimental import pallas as pl
from jax.experimental.pallas import tpu as pltpu

F32 = jnp.float32
BF16 = jnp.bfloat16

D_MODEL = 1024
D_FF = 2816
CHUNK = 64
CHUNK_SHIFT = CHUNK.bit_length() - 1
EPS = 1e-6
GLA_HEADS = 4
GLA_DK = 128
GLA_DV = 256
GLA_RANK = 16
GLA_TAU = 16.0
SWA_HEADS = 16
SWA_KV_HEADS = 2
SWA_HD = 64
SWA_GROUP = SWA_HEADS // SWA_KV_HEADS
WINDOW = 128
GLA_QK_W = GLA_HEADS * GLA_DK
GLA_V_W = GLA_HEADS * GLA_DV
SWA_Q_W = SWA_HEADS * SWA_HD
SWA_KV_W = SWA_KV_HEADS * SWA_HD

LANES = 128
RANK_PAD = LANES
FF_CHUNK = 256
N_FF_CHUNKS = D_FF // FF_CHUNK
VMEM_LIMIT = 56 * 1024 * 1024

_C_GQ = 0
_C_GK = _C_GQ + GLA_QK_W
_C_GLR = _C_GK + GLA_QK_W
_C_GV = _C_GLR + RANK_PAD
_C_GR = _C_GV + GLA_V_W
_C_SQ = _C_GR + GLA_V_W
_C_SK = _C_SQ + SWA_Q_W
_C_SV = _C_SK + SWA_KV_W
_C_GG = _C_SV + SWA_KV_W
_C_GS = _C_GG + D_MODEL
IN_COLS = _C_GS + D_MODEL


def _rmsnorm(x, g):
    ms = jnp.mean(x * x, axis=-1, keepdims=True)
    return x * lax.rsqrt(ms + EPS) * g


def _dot(a, b):
    return jnp.dot(a, b, preferred_element_type=F32)


def _dot_nt(a, b):
    return lax.dot_general(a, b, (((1,), (1,)), ((), ())), preferred_element_type=F32)


def _dot_tn(a, b):
    return lax.dot_general(a, b, (((0,), (0,)), ((), ())), preferred_element_type=F32)


def _ffn_kernel(x_ref, g_ref, w1_ref, w2_ref, gf_ref, o_ref, h_ref, acc_ref, *, final_norm):
    x = x_ref[...]
    h_ref[...] = _rmsnorm(x, g_ref[...]).astype(BF16)
    for c in range(N_FF_CHUNKS):
        cols = slice(c * FF_CHUNK, (c + 1) * FF_CHUNK)
        up_cols = slice(D_FF + c * FF_CHUNK, D_FF + (c + 1) * FF_CHUNK)
        h = h_ref[...]
        gate = _dot(h, w1_ref[:, cols])
        up = _dot(h, w1_ref[:, up_cols])
        act = (gate * jax.nn.sigmoid(gate) * up).astype(BF16)
        down = _dot(act, w2_ref[cols, :])
        if c == 0:
            acc_ref[...] = down
        else:
            acc_ref[...] += down
    y = x + 0.5 * acc_ref[...]
    if final_norm:
        y = _rmsnorm(y, gf_ref[...])
    o_ref[...] = y


def _resident(shape):
    nd = len(shape)
    return pl.BlockSpec(shape, lambda *_: (0,) * nd, pipeline_mode=pl.Buffered(1))


def _ffn(x, g, w1, w2, gf, *, final_norm, tm):
    n = x.shape[0]
    return pl.pallas_call(
        functools.partial(_ffn_kernel, final_norm=final_norm),
        grid=(n // tm,),
        in_specs=[
            pl.BlockSpec((tm, D_MODEL), lambda i: (i, 0)),
            _resident((1, D_MODEL)),
            _resident(w1.shape),
            _resident(w2.shape),
            _resident((1, D_MODEL)),
        ],
        out_specs=pl.BlockSpec((tm, D_MODEL), lambda i: (i, 0)),
        out_shape=jax.ShapeDtypeStruct((n, D_MODEL), F32),
        scratch_shapes=[pltpu.VMEM((tm, D_MODEL), BF16), pltpu.VMEM((tm, D_MODEL), F32)],
        compiler_params=pltpu.CompilerParams(
            dimension_semantics=("arbitrary",), vmem_limit_bytes=VMEM_LIMIT),
    )(x, g, w1, w2, gf)


def _ref_rows(p, m, row):
    n_rows, width = p.shape
    if m == 1:
        return jnp.where((row & 1) == 0, p, pltpu.roll(p, 1, 0))
    if m == 2:
        r = row & 3
        return jnp.where(r == 0, pltpu.roll(p, n_rows - 1, 0),
                         jnp.where(r == 1, p,
                                   jnp.where(r == 2, pltpu.roll(p, 1, 0), pltpu.roll(p, 2, 0))))
    nb = n_rows // (2 * m)
    p3 = p.reshape(nb, 2 * m, width)
    r3 = jnp.broadcast_to(p3[:, m - 1:m, :], (nb, 2 * m, width))
    return r3.reshape(n_rows, width)


def _gla_block(q, k, v, la, s_ref, gnorm):
    n_rows = q.shape[0]
    row = lax.broadcasted_iota(jnp.int32, (n_rows, 1), 0)
    levels = []
    m = 1
    while m < n_rows:
        levels.append(m)
        m *= 2
    p = la
    zs = []
    for m in levels:
        upper = (row & m) != 0
        r = _ref_rows(p, m, row)
        e = jnp.exp(jnp.where(upper, p, r - p))
        zs.append((jnp.where(upper, q, k) * e).astype(BF16))
        p = p + jnp.where(upper, r, 0.0)
    b = p
    b_last = b[n_rows - 1:n_rows, :]
    q_in = (q * jnp.exp(b)).astype(BF16)
    k_out = (k * jnp.exp(b_last - b)).astype(BF16)
    s_decay = jnp.exp(b_last)
    qb = q.astype(BF16)
    kb = k.astype(BF16)
    vb = v.astype(BF16)

    ti = lax.broadcasted_iota(jnp.int32, (n_rows, n_rows), 0)
    si = lax.broadcasted_iota(jnp.int32, (n_rows, n_rows), 1)
    split = jnp.where(ti > si, ti ^ si, 0)
    split_level = 31 - lax.clz(split)
    eye = ti == si

    outs = []
    for h in range(GLA_HEADS):
        ks = slice(h * GLA_DK, (h + 1) * GLA_DK)
        vs = slice(h * GLA_DV, (h + 1) * GLA_DV)
        a = jnp.where(eye, _dot_nt(qb[:, ks], kb[:, ks]), 0.0)
        for li, z in enumerate(zs):
            a = jnp.where(split_level == li, _dot_nt(z[:, ks], z[:, ks]), a)
        s_old = s_ref[h]
        o = _dot(a.astype(BF16), vb[:, vs]) + _dot(q_in[:, ks], s_old.astype(BF16))
        dcol = jnp.transpose(jnp.broadcast_to(s_decay[:, ks], (8, GLA_DK)))[:, 0:1]
        s_ref[h] = dcol * s_old + _dot_tn(k_out[:, ks], vb[:, vs])
        outs.append(_rmsnorm(o, gnorm))
    return jnp.concatenate(outs, axis=1)


def _swa_block(q, kn, ks, vn, vs, mask, sink_ref):
    tq = q.shape[0]
    nk = kn.shape[0]
    low = lax.broadcasted_iota(jnp.int32, (nk, LANES), 1) < SWA_HD
    low_o = lax.broadcasted_iota(jnp.int32, (tq, LANES), 1) < SWA_HD
    pairs = SWA_GROUP // 2
    out_tiles = [None] * (SWA_HEADS // 2)
    for g in range(SWA_KV_HEADS):
        k_lo, k_hi = (kn, ks) if g == 0 else (ks, kn)
        v_lo, v_hi = (vn, vs) if g == 0 else (vs, vn)
        kbd = jnp.concatenate([jnp.where(low, k_lo, 0.0), jnp.where(low, 0.0, k_hi)], axis=0).astype(BF16)
        vbd = jnp.concatenate([jnp.where(low, v_lo, 0.0), jnp.where(low, 0.0, v_hi)], axis=0).astype(BF16)
        qs = jnp.concatenate(
            [q[:, (g * pairs + i) * LANES:(g * pairs + i + 1) * LANES] for i in range(pairs)], axis=0).astype(BF16)
        s = _dot_nt(qs, kbd)
        p_rows, inv_rows = [], []
        for i in range(pairs):
            p_half, inv_half = [], []
            for half in range(2):
                sk = sink_ref[g * SWA_GROUP + 2 * i + half]
                sh = jnp.where(mask, s[i * tq:(i + 1) * tq, half * nk:(half + 1) * nk], -jnp.inf)
                mx = jnp.maximum(jnp.max(sh, axis=-1, keepdims=True), sk)
                ph = jnp.exp(sh - mx)
                den = jnp.sum(ph, axis=-1, keepdims=True) + jnp.exp(sk - mx)
                p_half.append(ph.astype(BF16))
                inv_half.append(1.0 / den)
            p_rows.append(jnp.concatenate(p_half, axis=1))
            inv_rows.append(jnp.where(low_o, inv_half[0], inv_half[1]))
        o2 = _dot(jnp.concatenate(p_rows, axis=0), vbd)
        for i in range(pairs):
            out_tiles[g * pairs + i] = o2[i * tq:(i + 1) * tq, :] * inv_rows[i]
    return jnp.concatenate(out_tiles, axis=1)


def _mix_kernel(*refs, is_prompt, tile, gla_block):
    if is_prompt:
        (sink_ref, x_ref, g_ref, win_ref, wgu_ref, bg_ref, gn_ref, wbg_ref, wbs_ref, wout_ref,
         y_ref, s_ref, ck_ref, cv_ref, kx_ref, vx_ref) = refs
    else:
        (sink_ref, x_ref, g_ref, win_ref, wgu_ref, bg_ref, gn_ref, wbg_ref, wbs_ref, wout_ref,
         s0_ref, ck0_ref, cv0_ref, y_ref, s_ref, ck_ref, cv_ref, kx_ref, vx_ref) = refs
    step = pl.program_id(1)

    if is_prompt:
        @pl.when(step == 0)
        def _():
            s_ref[...] = jnp.zeros_like(s_ref)
            ck_ref[...] = jnp.zeros_like(ck_ref)
            cv_ref[...] = jnp.zeros_like(cv_ref)
    else:
        s_ref[...] = s0_ref[...]
        ck_ref[...] = ck0_ref[...]
        cv_ref[...] = cv0_ref[...]

    x = x_ref[...]
    h = _rmsnorm(x, g_ref[...]).astype(BF16)

    pa = _dot(h, win_ref[:, _C_GQ:_C_GV])
    gv = _dot(h, win_ref[:, _C_GV:_C_GR])
    logit = _dot(pa[:, _C_GLR:_C_GLR + RANK_PAD].astype(BF16), wgu_ref[...]) + bg_ref[...]
    log_alpha = (jnp.minimum(logit, 0.0) - jnp.log1p(jnp.exp(-jnp.abs(logit)))) * (1.0 / GLA_TAU)
    gq = pa[:, _C_GQ:_C_GQ + GLA_QK_W] * (GLA_DK ** -0.5)
    gk = pa[:, _C_GK:_C_GK + GLA_QK_W]
    gn = gn_ref[...]
    o_parts = []
    for c in range(tile // gla_block):
        rs = slice(c * gla_block, (c + 1) * gla_block)
        o_parts.append(_gla_block(gq[rs], gk[rs], gv[rs], log_alpha[rs], s_ref, gn))
    o_gla = o_parts[0] if len(o_parts) == 1 else jnp.concatenate(o_parts, axis=0)
    gr = _dot(h, win_ref[:, _C_GR:_C_SQ])
    o_gla = (o_gla * (gr * jax.nn.sigmoid(gr))).astype(BF16)

    pb = _dot(h, win_ref[:, _C_SQ:_C_GG])
    sq = pb[:, :SWA_Q_W] * (SWA_HD ** -0.5)
    sk = pb[:, SWA_Q_W:SWA_Q_W + SWA_KV_W]
    sv = pb[:, SWA_Q_W + SWA_KV_W:]
    kx_ref[0:WINDOW, :] = ck_ref[...]
    vx_ref[0:WINDOW, :] = cv_ref[...]
    kx_ref[WINDOW:WINDOW + tile, :] = sk
    vx_ref[WINDOW:WINDOW + tile, :] = sv
    n_ext = kx_ref.shape[0]
    if WINDOW + tile < n_ext:
        kx_ref[WINDOW + tile:, :] = jnp.zeros((n_ext - WINDOW - tile, LANES), F32)
        vx_ref[WINDOW + tile:, :] = jnp.zeros((n_ext - WINDOW - tile, LANES), F32)
    ck_ref[...] = kx_ref[tile:tile + WINDOW, :]
    cv_ref[...] = vx_ref[tile:tile + WINDOW, :]

    nk = 2 * WINDOW
    if is_prompt:
        tq = 2 * CHUNK
        qi = lax.broadcasted_iota(jnp.int32, (tq, nk), 0) >> CHUNK_SHIFT
        kj = lax.broadcasted_iota(jnp.int32, (tq, nk), 1)
        kc = kj >> CHUNK_SHIFT
        band = (kc >= qi) & (kc <= qi + WINDOW // CHUNK)
        o_rows = []
        for pg in range(tile // tq):
            kx = kx_ref[pg * tq:pg * tq + nk, :]
            vx = vx_ref[pg * tq:pg * tq + nk, :]
            if pg == 0:
                mask = band & ((kj >= WINDOW) | (step > 0))
            else:
                mask = band
            o_rows.append(_swa_block(sq[pg * tq:(pg + 1) * tq], kx, pltpu.roll(kx, SWA_HD, 1),
                                     vx, pltpu.roll(vx, SWA_HD, 1), mask, sink_ref))
        o_swa = jnp.concatenate(o_rows, axis=0)
    else:
        kj = lax.broadcasted_iota(jnp.int32, (tile, nk), 1)
        kx = kx_ref[...]
        vx = vx_ref[...]
        o_swa = _swa_block(sq, kx, pltpu.roll(kx, SWA_HD, 1), vx, pltpu.roll(vx, SWA_HD, 1),
                           kj < WINDOW + tile, sink_ref)
    o_swa = o_swa.astype(BF16)

    gg = _dot(h, win_ref[:, _C_GG:_C_GS])
    gs = _dot(h, win_ref[:, _C_GS:IN_COLS])
    merged = jax.nn.sigmoid(gg) * _dot(o_gla, wbg_ref[...]) + jax.nn.sigmoid(gs) * _dot(o_swa, wbs_ref[...])
    y_ref[...] = x + _dot(merged.astype(BF16), wout_ref[...])


def _mix(x, sinks, g, win, wgu, bg, gn, wbg, wbs, wout, state0=None, ck0=None, cv0=None, *, tile, gla_block):
    batch, seq, _ = x.shape
    is_prompt = state0 is None
    n_ext = WINDOW + tile if is_prompt else 2 * WINDOW
    in_specs = [
        pl.BlockSpec(memory_space=pltpu.SMEM),
        pl.BlockSpec((None, tile, D_MODEL), lambda b, i: (b, i, 0)),
        _resident((1, D_MODEL)),
        _resident(win.shape),
        _resident(wgu.shape),
        _resident((1, GLA_QK_W)),
        _resident((1, GLA_DV)),
        _resident(wbg.shape),
        _resident(wbs.shape),
        _resident(wout.shape),
    ]
    args = [sinks, x, g, win, wgu, bg, gn, wbg, wbs, wout]
    state_spec = pl.BlockSpec((None, GLA_HEADS, GLA_DK, GLA_DV), lambda b, i: (b, 0, 0, 0))
    cache_spec = pl.BlockSpec((None, WINDOW, SWA_KV_W), lambda b, i: (b, 0, 0))
    if not is_prompt:
        in_specs += [state_spec, cache_spec, cache_spec]
        args += [state0, ck0, cv0]
    return pl.pallas_call(
        functools.partial(_mix_kernel, is_prompt=is_prompt, tile=tile, gla_block=gla_block),
        grid=(batch, seq // tile),
        in_specs=in_specs,
        out_specs=[
            pl.BlockSpec((None, tile, D_MODEL), lambda b, i: (b, i, 0)),
            state_spec, cache_spec, cache_spec,
        ],
        out_shape=[
            jax.ShapeDtypeStruct((batch, seq, D_MODEL), F32),
            jax.ShapeDtypeStruct((batch, GLA_HEADS, GLA_DK, GLA_DV), F32),
            jax.ShapeDtypeStruct((batch, WINDOW, SWA_KV_W), F32),
            jax.ShapeDtypeStruct((batch, WINDOW, SWA_KV_W), F32),
        ],
        scratch_shapes=[pltpu.VMEM((n_ext, LANES), F32), pltpu.VMEM((n_ext, LANES), F32)],
        compiler_params=pltpu.CompilerParams(
            dimension_semantics=("arbitrary", "arbitrary"), vmem_limit_bytes=VMEM_LIMIT),
    )(*args)


def _ffn_weights(norm_g, w_in, w_out):
    return norm_g.reshape(1, D_MODEL), w_in.astype(BF16), w_out.astype(BF16)


def _mix_weights(w_in, w_gate_up):
    o = 0
    parts = {}
    for name, width in (("gq", GLA_QK_W), ("gk", GLA_QK_W), ("gv", GLA_V_W), ("gr", GLA_V_W),
                        ("glr", GLA_RANK), ("sq", SWA_Q_W), ("sk", SWA_KV_W), ("sv", SWA_KV_W),
                        ("gg", D_MODEL), ("gs", D_MODEL)):
        parts[name] = w_in[:, o:o + width]
        o += width
    glr = jnp.pad(parts["glr"], ((0, 0), (0, RANK_PAD - GLA_RANK)))
    win = jnp.concatenate([parts["gq"], parts["gk"], glr, parts["gv"], parts["gr"], parts["sq"],
                           parts["sk"], parts["sv"], parts["gg"], parts["gs"]], axis=1).astype(BF16)
    wgu = jnp.pad(w_gate_up, ((0, RANK_PAD - GLA_RANK), (0, 0))).astype(BF16)
    return win, wgu


def _layer(x, ffn1, mixw, ffn2, gf, state0=None, ck0=None, cv0=None, *, tm, tile, gla_block):
    batch, seq, _ = x.shape
    x1 = _ffn(x.reshape(batch * seq, D_MODEL), *ffn1, gf, final_norm=False, tm=tm)
    x2, state, ck, cv = _mix(x1.reshape(batch, seq, D_MODEL), *mixw, state0, ck0, cv0,
                             tile=tile, gla_block=gla_block)
    y = _ffn(x2.reshape(batch * seq, D_MODEL), *ffn2, gf, final_norm=True, tm=tm)
    return (y.reshape(batch, seq, D_MODEL), state[None],
            ck.reshape(1, batch, WINDOW, SWA_KV_HEADS, SWA_HD),
            cv.reshape(1, batch, WINDOW, SWA_KV_HEADS, SWA_HD))


def kernel(x_prompt, x_sample, state_gla, cache_swa_k, cache_swa_v, norm_ffn1, w_ffn1_in, w_ffn1_out, norm_mix, w_in, w_gla_gate_up, b_gla_gate, gla_norm, swa_sinks, w_branch_gla, w_branch_swa, w_out, norm_ffn2, w_ffn2_in, w_ffn2_out, norm_final):
    ffn1 = _ffn_weights(norm_ffn1[0], w_ffn1_in[0], w_ffn1_out[0])
    ffn2 = _ffn_weights(norm_ffn2[0], w_ffn2_in[0], w_ffn2_out[0])
    win, wgu = _mix_weights(w_in[0], w_gla_gate_up[0])
    mixw = (swa_sinks[0], norm_mix[0].reshape(1, D_MODEL), win, wgu, b_gla_gate[0].reshape(1, GLA_QK_W),
            gla_norm[0].reshape(1, GLA_DV), w_branch_gla[0].astype(BF16), w_branch_swa[0].astype(BF16),
            w_out[0].astype(BF16))
    gf = norm_final.reshape(1, D_MODEL)
    dec_batch, dec_seq, _ = x_sample.shape
    yp, sg_p, sk_p, sv_p = _layer(x_prompt, ffn1, mixw, ffn2, gf, tm=512, tile=256, gla_block=128)
    ys, sg_s, sk_s, sv_s = _layer(
        x_sample, ffn1, mixw, ffn2, gf, state_gla[0],
        cache_swa_k[0].reshape(dec_batch, WINDOW, SWA_KV_W), cache_swa_v[0].reshape(dec_batch, WINDOW, SWA_KV_W),
        tm=dec_batch * dec_seq, tile=dec_seq, gla_block=dec_seq)
    return yp, ys, sg_p, sk_p, sv_p, sg_s, sk_s, sv_s
```

```python
import functools

import jax
import jax.numpy as jnp
from jax import lax
from jax.experimental import pallas as pl
from jax.experimental.pallas import tpu as pltpu

F32 = jnp.float32
BF16 = jnp.bfloat16

D_MODEL = 1024
D_FF = 2816
CHUNK = 64
CHUNK_SHIFT = CHUNK.bit_length() - 1
EPS = 1e-6
GLA_HEADS = 4
GLA_DK = 128
GLA_DV = 256
GLA_RANK = 16
GLA_TAU = 16.0
SWA_HEADS = 16
SWA_KV_HEADS = 2
SWA_HD = 64
SWA_GROUP = SWA_HEADS // SWA_KV_HEADS
WINDOW = 128
GLA_QK_W = GLA_HEADS * GLA_DK
GLA_V_W = GLA_HEADS * GLA_DV
SWA_Q_W = SWA_HEADS * SWA_HD
SWA_KV_W = SWA_KV_HEADS * SWA_HD

LANES = 128
RANK_PAD = LANES
FF_CHUNK = 256
N_FF_CHUNKS = D_FF // FF_CHUNK
VMEM_LIMIT = 56 * 1024 * 1024

FFN_ROWS = 512
MIX_TILE = 256
GLA_BLOCK = 128
SWA_Q_ROWS = 2 * CHUNK
SWA_KEYS = 2 * WINDOW

_C_GQ = 0
_C_GK = _C_GQ + GLA_QK_W
_C_GLR = _C_GK + GLA_QK_W
_C_GV = _C_GLR + RANK_PAD
_C_GR = _C_GV + GLA_V_W
_C_SQ = _C_GR + GLA_V_W
_C_SK = _C_SQ + SWA_Q_W
_C_SV = _C_SK + SWA_KV_W
_C_GG = _C_SV + SWA_KV_W
_C_GS = _C_GG + D_MODEL
IN_COLS = _C_GS + D_MODEL

_PROJ_LAYOUT = (
    (_C_GQ, _C_GV - _C_GQ, F32, None),
    (_C_GV, GLA_V_W, BF16, None),
    (_C_GR, GLA_V_W, F32, None),
    (_C_SQ, SWA_Q_W, BF16, SWA_HD ** -0.5),
    (_C_SK, 2 * SWA_KV_W, F32, None),
    (_C_GG, D_MODEL, F32, None),
    (_C_GS, D_MODEL, F32, None),
)
PROJ_TASK_COLS = 256


def _rmsnorm(x, g):
    ms = jnp.mean(x * x, axis=-1, keepdims=True)
    return x * lax.rsqrt(ms + EPS) * g


def _dot(a, b):
    return jnp.dot(a, b, preferred_element_type=F32)


def _dot_nt(a, b):
    return lax.dot_general(a, b, (((1,), (1,)), ((), ())), preferred_element_type=F32)


def _dot_tn(a, b):
    return lax.dot_general(a, b, (((0,), (0,)), ((), ())), preferred_element_type=F32)


def _ffn_kernel(x_ref, g_ref, w1_ref, w2_ref, gf_ref, o_ref, h_ref, acc_ref, *, final_norm):
    x = x_ref[...]
    h_ref[...] = _rmsnorm(x, g_ref[...]).astype(BF16)
    for c in range(N_FF_CHUNKS):
        cols = slice(c * FF_CHUNK, (c + 1) * FF_CHUNK)
        up_cols = slice(D_FF + c * FF_CHUNK, D_FF + (c + 1) * FF_CHUNK)
        h = h_ref[...]
        gate = _dot(h, w1_ref[:, cols])
        up = _dot(h, w1_ref[:, up_cols])
        act = (gate * jax.nn.sigmoid(gate) * up).astype(BF16)
        down = _dot(act, w2_ref[cols, :])
        if c == 0:
            acc_ref[...] = down
        else:
            acc_ref[...] += down
    y = x + 0.5 * acc_ref[...]
    if final_norm:
        y = _rmsnorm(y, gf_ref[...])
    o_ref[...] = y


def _resident(shape):
    nd = len(shape)
    return pl.BlockSpec(shape, lambda *_: (0,) * nd, pipeline_mode=pl.Buffered(1))


def _ffn(x, g, w1, w2, gf, *, final_norm):
    n = x.shape[0]
    tm = min(FFN_ROWS, n)
    return pl.pallas_call(
        functools.partial(_ffn_kernel, final_norm=final_norm),
        grid=(n // tm,),
        in_specs=[
            pl.BlockSpec((tm, D_MODEL), lambda i: (i, 0)),
            _resident((1, D_MODEL)),
            _resident(w1.shape),
            _resident(w2.shape),
            _resident((1, D_MODEL)),
        ],
        out_specs=pl.BlockSpec((tm, D_MODEL), lambda i: (i, 0)),
        out_shape=jax.ShapeDtypeStruct((n, D_MODEL), F32),
        scratch_shapes=[pltpu.VMEM((tm, D_MODEL), BF16), pltpu.VMEM((tm, D_MODEL), F32)],
        compiler_params=pltpu.CompilerParams(
            dimension_semantics=("arbitrary",), vmem_limit_bytes=VMEM_LIMIT),
    )(x, g, w1, w2, gf)


def _proj_tasks(h_ref, win_ref, p_refs):
    tasks = []
    for dst, (col0, width, dtype, scale) in zip(p_refs, _PROJ_LAYOUT):
        for c in range(0, width, PROJ_TASK_COLS):
            w = min(PROJ_TASK_COLS, width - c)

            def task(dst=dst, c=c, w=w, col0=col0, dtype=dtype, scale=scale):
                v = _dot(h_ref[...], win_ref[:, col0 + c:col0 + c + w])
                if scale is not None:
                    v = v * scale
                dst[:, c:c + w] = v.astype(dtype)

            tasks.append(task)
    return tasks


class _Interleave:
    def __init__(self, tasks=()):
        self._tasks = list(tasks)

    def __call__(self, n=1):
        for _ in range(n):
            if self._tasks:
                self._tasks.pop(0)()

    def flush(self):
        self(len(self._tasks))


def _ref_rows(p, m, row):
    n_rows, width = p.shape
    if m == 1:
        return jnp.where((row & 1) == 0, p, pltpu.roll(p, 1, 0))
    if m == 2:
        r = row & 3
        return jnp.where(r == 0, pltpu.roll(p, n_rows - 1, 0),
                         jnp.where(r == 1, p,
                                   jnp.where(r == 2, pltpu.roll(p, 1, 0), pltpu.roll(p, 2, 0))))
    nb = n_rows // (2 * m)
    p3 = p.reshape(nb, 2 * m, width)
    r3 = jnp.broadcast_to(p3[:, m - 1:m, :], (nb, 2 * m, width))
    return r3.reshape(n_rows, width)


def _gla_scan(q, k, la, fill):
    n_rows = q.shape[0]
    row = lax.broadcasted_iota(jnp.int32, (n_rows, 1), 0)
    p = la
    zs = []
    m = 1
    while m < n_rows:
        upper = (row & m) != 0
        r = _ref_rows(p, m, row)
        e = jnp.exp(jnp.where(upper, p, r - p))
        zs.append((jnp.where(upper, q, k) * e).astype(BF16))
        p = p + jnp.where(upper, r, 0.0)
        fill()
        m *= 2
    b = p
    b_last = b[n_rows - 1:n_rows, :]
    return dict(zs=zs, qb=q.astype(BF16), kb=k.astype(BF16),
                q_in=(q * jnp.exp(b)).astype(BF16), k_out=(k * jnp.exp(b_last - b)).astype(BF16),
                s_decay=jnp.exp(b_last))


def _gla_scores(sc, fill):
    n_rows = sc["qb"].shape[0]
    ti = lax.broadcasted_iota(jnp.int32, (n_rows, n_rows), 0)
    si = lax.broadcasted_iota(jnp.int32, (n_rows, n_rows), 1)
    split_level = 31 - lax.clz(jnp.where(ti > si, ti ^ si, 0))
    eye = ti == si
    a_heads = []
    for h in range(GLA_HEADS):
        ks = slice(h * GLA_DK, (h + 1) * GLA_DK)
        a = jnp.where(eye, _dot_nt(sc["qb"][:, ks], sc["kb"][:, ks]), 0.0)
        for li, z in enumerate(sc["zs"]):
            a = jnp.where(split_level == li, _dot_nt(z[:, ks], z[:, ks]), a)
        a_heads.append(a.astype(BF16))
        fill()
    return a_heads


def _gla_out(sc, a_heads, vb, s_ref, sb_ref, gnorm):
    outs = []
    for h in range(GLA_HEADS):
        ks = slice(h * GLA_DK, (h + 1) * GLA_DK)
        vs = slice(h * GLA_DV, (h + 1) * GLA_DV)
        o = _dot(a_heads[h], vb[:, vs]) + _dot(sc["q_in"][:, ks], sb_ref[h])
        dcol = jnp.transpose(jnp.broadcast_to(sc["s_decay"][:, ks], (8, GLA_DK)))[:, 0:1]
        s_new = dcol * s_ref[h] + _dot_tn(sc["k_out"][:, ks], vb[:, vs])
        s_ref[h] = s_new
        sb_ref[h] = s_new.astype(BF16)
        outs.append(_rmsnorm(o, gnorm))
    return jnp.concatenate(outs, axis=1)


def _gla_inputs(qkg, wgu_ref, bg_ref):
    logit = _dot(qkg[:, _C_GLR:_C_GLR + RANK_PAD].astype(BF16), wgu_ref[...]) + bg_ref[...]
    log_alpha = (jnp.minimum(logit, 0.0) - jnp.log1p(jnp.exp(-jnp.abs(logit)))) * (1.0 / GLA_TAU)
    gq = qkg[:, _C_GQ:_C_GQ + GLA_QK_W] * (GLA_DK ** -0.5)
    gk = qkg[:, _C_GK:_C_GK + GLA_QK_W]
    return gq, gk, log_alpha


def _swa_scores(qb, kn, ks, vn, vs):
    nk = kn.shape[0]
    low = lax.broadcasted_iota(jnp.int32, (nk, LANES), 1) < SWA_HD
    pairs = SWA_GROUP // 2
    res = []
    for g in range(SWA_KV_HEADS):
        k_lo, k_hi = (kn, ks) if g == 0 else (ks, kn)
        v_lo, v_hi = (vn, vs) if g == 0 else (vs, vn)
        kbd = jnp.concatenate([jnp.where(low, k_lo, 0.0), jnp.where(low, 0.0, k_hi)], axis=0).astype(BF16)
        vbd = jnp.concatenate([jnp.where(low, v_lo, 0.0), jnp.where(low, 0.0, v_hi)], axis=0).astype(BF16)
        qs = jnp.concatenate(
            [qb[:, (g * pairs + i) * LANES:(g * pairs + i + 1) * LANES] for i in range(pairs)], axis=0)
        res.append((_dot_nt(qs, kbd), vbd))
    return res


def _swa_softmax(scores, mask, sink_ref, fill):
    tq, nk = mask.shape
    low_o = lax.broadcasted_iota(jnp.int32, (tq, LANES), 1) < SWA_HD
    pairs = SWA_GROUP // 2
    res = []
    for g, (s, _) in enumerate(scores):
        p_rows, inv_rows = [], []
        for i in range(pairs):
            p_half, inv_half = [], []
            for half in range(2):
                sk = sink_ref[g * SWA_GROUP + 2 * i + half]
                sh = jnp.where(mask, s[i * tq:(i + 1) * tq, half * nk:(half + 1) * nk], -jnp.inf)
                mx = jnp.maximum(jnp.max(sh, axis=-1, keepdims=True), sk)
                ph = jnp.exp(sh - mx)
                den = jnp.sum(ph, axis=-1, keepdims=True) + jnp.exp(sk - mx)
                p_half.append(ph.astype(BF16))
                inv_half.append(1.0 / den)
            p_rows.append(jnp.concatenate(p_half, axis=1))
            inv_rows.append(jnp.where(low_o, inv_half[0], inv_half[1]))
            fill()
        res.append((jnp.concatenate(p_rows, axis=0), inv_rows))
    return res


def _swa_out(scores, probs):
    pairs = SWA_GROUP // 2
    tiles = []
    for (_, vbd), (p, inv_rows) in zip(scores, probs):
        tq = p.shape[0] // pairs
        o2 = _dot(p, vbd)
        tiles += [o2[i * tq:(i + 1) * tq, :] * inv_rows[i] for i in range(pairs)]
    return jnp.concatenate(tiles, axis=1).astype(BF16)


def _swa_window(kx_ref, vx_ref, r0):
    kx = kx_ref[r0:r0 + SWA_KEYS, :]
    vx = vx_ref[r0:r0 + SWA_KEYS, :]
    return kx, pltpu.roll(kx, SWA_HD, 1), vx, pltpu.roll(vx, SWA_HD, 1)


def _prompt_mask(history_valid):
    qi = lax.broadcasted_iota(jnp.int32, (SWA_Q_ROWS, SWA_KEYS), 0) >> CHUNK_SHIFT
    kj = lax.broadcasted_iota(jnp.int32, (SWA_Q_ROWS, SWA_KEYS), 1)
    kc = kj >> CHUNK_SHIFT
    band = (kc >= qi) & (kc <= qi + WINDOW // CHUNK)
    if history_valid is None:
        return band
    return band & ((kj >= WINDOW) | history_valid)


def _merge(x, o_gla, o_swa, gg, gs, wbg_ref, wbs_ref, wout_ref):
    merged = jax.nn.sigmoid(gg) * _dot(o_gla, wbg_ref[...]) + jax.nn.sigmoid(gs) * _dot(o_swa, wbs_ref[...])
    return x + _dot(merged.astype(BF16), wout_ref[...])


def _mix_tile(p_tile, x, kx_ref, vx_ref, row0, history_valid, refs, fill):
    sink_ref, wgu_ref, bg_ref, gn_ref, wbg_ref, wbs_ref, wout_ref, s_ref, sb_ref = refs
    p_qkg, p_gv, p_gr, p_sq, _, p_gg, p_gs = p_tile
    gq, gk, log_alpha = _gla_inputs(p_qkg[...], wgu_ref, bg_ref)
    gn = gn_ref[...]
    blocks = [slice(c * GLA_BLOCK, (c + 1) * GLA_BLOCK) for c in range(MIX_TILE // GLA_BLOCK)]
    groups = [slice(c * SWA_Q_ROWS, (c + 1) * SWA_Q_ROWS) for c in range(MIX_TILE // SWA_Q_ROWS)]
    assert len(blocks) == len(groups)
    o_gla, o_swa = [], []
    for c, (rs, qs) in enumerate(zip(blocks, groups)):
        sc = _gla_scan(gq[rs], gk[rs], log_alpha[rs], fill)
        scores = _swa_scores(p_sq[qs, :], *_swa_window(kx_ref, vx_ref, row0 + c * SWA_Q_ROWS))
        a_heads = _gla_scores(sc, fill)
        probs = _swa_softmax(scores, _prompt_mask(history_valid if c == 0 else None), sink_ref, fill)
        o_gla.append(_gla_out(sc, a_heads, p_gv[rs, :], s_ref, sb_ref, gn))
        o_swa.append(_swa_out(scores, probs))
    fill.flush()
    gr = p_gr[...]
    og = (jnp.concatenate(o_gla, axis=0) * (gr * jax.nn.sigmoid(gr))).astype(BF16)
    return _merge(x, og, jnp.concatenate(o_swa, axis=0), p_gg[...], p_gs[...], wbg_ref, wbs_ref, wout_ref)


def _mix_prompt_kernel(sink_ref, x_ref, xn_ref, g_ref, win_ref, wgu_ref, bg_ref, gn_ref, wbg_ref, wbs_ref,
                       wout_ref, y_ref, s_ref, ck_ref, cv_ref, kx_ref, vx_ref, sb_ref, h0_ref, h1_ref, *p_refs,
                       steps_per_seq):
    step = pl.program_id(0)
    seq_step = step % steps_per_seq
    t = MIX_TILE
    n_pieces = len(_PROJ_LAYOUT)
    p_tiles = (p_refs[:n_pieces], p_refs[n_pieces:])
    h_refs = (h0_ref, h1_ref)
    refs = (sink_ref, wgu_ref, bg_ref, gn_ref, wbg_ref, wbs_ref, wout_ref, s_ref, sb_ref)

    @pl.when(step == 0)
    def _():
        h0_ref[...] = _rmsnorm(x_ref[0:t, :], g_ref[...]).astype(BF16)
        _Interleave(_proj_tasks(h0_ref, win_ref, p_tiles[0])).flush()

    @pl.when(seq_step == 0)
    def _():
        s_ref[...] = jnp.zeros_like(s_ref)
        sb_ref[...] = jnp.zeros_like(sb_ref)
        ck_ref[...] = jnp.zeros_like(ck_ref)
        cv_ref[...] = jnp.zeros_like(cv_ref)

    kx_ref[0:WINDOW, :] = ck_ref[...]
    vx_ref[0:WINDOW, :] = cv_ref[...]
    for i in range(2):
        p_skv = p_tiles[i][4]
        x_ahead = x_ref[t:2 * t, :] if i == 0 else xn_ref[...]
        h_refs[1 - i][...] = _rmsnorm(x_ahead, g_ref[...]).astype(BF16)
        fill = _Interleave(_proj_tasks(h_refs[1 - i], win_ref, p_tiles[1 - i]))
        kx_ref[WINDOW + i * t:WINDOW + (i + 1) * t, :] = p_skv[:, :SWA_KV_W]
        vx_ref[WINDOW + i * t:WINDOW + (i + 1) * t, :] = p_skv[:, SWA_KV_W:]
        y_ref[i * t:(i + 1) * t, :] = _mix_tile(p_tiles[i], x_ref[i * t:(i + 1) * t, :], kx_ref, vx_ref, i * t,
                                                (seq_step > 0) if i == 0 else None, refs, fill)
    ck_ref[...] = kx_ref[2 * t:2 * t + WINDOW, :]
    cv_ref[...] = vx_ref[2 * t:2 * t + WINDOW, :]


def _mix_sample_kernel(sink_ref, x_ref, g_ref, win_ref, wgu_ref, bg_ref, gn_ref, wbg_ref, wbs_ref, wout_ref,
                       s0_ref, ck0_ref, cv0_ref, y_ref, s_ref, ck_ref, cv_ref, kx_ref, vx_ref, sb_ref, h_ref,
                       *p_refs):
    x = x_ref[...]
    t = x.shape[0]
    s_ref[...] = s0_ref[...]
    sb_ref[...] = s0_ref[...].astype(BF16)
    h_ref[...] = _rmsnorm(x, g_ref[...]).astype(BF16)
    _Interleave(_proj_tasks(h_ref, win_ref, p_refs)).flush()
    p_qkg, p_gv, p_gr, p_sq, p_skv, p_gg, p_gs = p_refs
    no_fill = _Interleave()

    gq, gk, log_alpha = _gla_inputs(p_qkg[...], wgu_ref, bg_ref)
    sc = _gla_scan(gq, gk, log_alpha, no_fill)
    o_gla = _gla_out(sc, _gla_scores(sc, no_fill), p_gv[...], s_ref, sb_ref, gn_ref[...])
    gr = p_gr[...]
    o_gla = (o_gla * (gr * jax.nn.sigmoid(gr))).astype(BF16)

    kx_ref[0:WINDOW, :] = ck0_ref[...]
    vx_ref[0:WINDOW, :] = cv0_ref[...]
    kx_ref[WINDOW:WINDOW + t, :] = p_skv[:, :SWA_KV_W]
    vx_ref[WINDOW:WINDOW + t, :] = p_skv[:, SWA_KV_W:]
    kx_ref[WINDOW + t:, :] = jnp.zeros((SWA_KEYS - WINDOW - t, LANES), F32)
    vx_ref[WINDOW + t:, :] = jnp.zeros((SWA_KEYS - WINDOW - t, LANES), F32)
    ck_ref[...] = kx_ref[t:t + WINDOW, :]
    cv_ref[...] = vx_ref[t:t + WINDOW, :]
    kj = lax.broadcasted_iota(jnp.int32, (t, SWA_KEYS), 1)
    scores = _swa_scores(p_sq[...], *_swa_window(kx_ref, vx_ref, 0))
    o_swa = _swa_out(scores, _swa_softmax(scores, kj < WINDOW + t, sink_ref, no_fill))
    y_ref[...] = _merge(x, o_gla, o_swa, p_gg[...], p_gs[...], wbg_ref, wbs_ref, wout_ref)


def _mix_weight_specs(win, wgu, wbg, wbs, wout):
    return [
        _resident((1, D_MODEL)),
        _resident(win.shape),
        _resident(wgu.shape),
        _resident((1, GLA_QK_W)),
        _resident((1, GLA_DV)),
        _resident(wbg.shape),
        _resident(wbs.shape),
        _resident(wout.shape),
    ]


def _mix_out_shapes(batch, seq):
    return [
        jax.ShapeDtypeStruct((batch * seq, D_MODEL), F32),
        jax.ShapeDtypeStruct((batch, GLA_HEADS, GLA_DK, GLA_DV), F32),
        jax.ShapeDtypeStruct((batch, WINDOW, SWA_KV_W), F32),
        jax.ShapeDtypeStruct((batch, WINDOW, SWA_KV_W), F32),
    ]


def _mix_scratch(key_rows):
    return [pltpu.VMEM((key_rows, LANES), F32), pltpu.VMEM((key_rows, LANES), F32),
            pltpu.VMEM((GLA_HEADS, GLA_DK, GLA_DV), BF16)]


def _mix_prompt(x, batch, seq, sinks, g, win, wgu, bg, gn, wbg, wbs, wout):
    rows = 2 * MIX_TILE
    steps_per_seq = seq // rows
    n_steps = batch * steps_per_seq
    last_tile = batch * seq // MIX_TILE - 1
    state_spec = pl.BlockSpec((None, GLA_HEADS, GLA_DK, GLA_DV), lambda i: (i // steps_per_seq, 0, 0, 0))
    cache_spec = pl.BlockSpec((None, WINDOW, SWA_KV_W), lambda i: (i // steps_per_seq, 0, 0))
    return pl.pallas_call(
        functools.partial(_mix_prompt_kernel, steps_per_seq=steps_per_seq),
        grid=(n_steps,),
        in_specs=[
            pl.BlockSpec(memory_space=pltpu.SMEM),
            pl.BlockSpec((rows, D_MODEL), lambda i: (i, 0)),
            pl.BlockSpec((MIX_TILE, D_MODEL), lambda i: (jnp.minimum(2 * i + 2, last_tile), 0)),
        ] + _mix_weight_specs(win, wgu, wbg, wbs, wout),
        out_specs=[pl.BlockSpec((rows, D_MODEL), lambda i: (i, 0)), state_spec, cache_spec, cache_spec],
        out_shape=_mix_out_shapes(batch, seq),
        scratch_shapes=_mix_scratch(WINDOW + rows)
        + [pltpu.VMEM((MIX_TILE, D_MODEL), BF16)] * 2
        + [pltpu.VMEM((MIX_TILE, w), dt) for _, w, dt, _ in _PROJ_LAYOUT] * 2,
        compiler_params=pltpu.CompilerParams(
            dimension_semantics=("arbitrary",), vmem_limit_bytes=VMEM_LIMIT),
    )(sinks, x, x, g, win, wgu, bg, gn, wbg, wbs, wout)


def _mix_sample(x, batch, seq, sinks, g, win, wgu, bg, gn, wbg, wbs, wout, state0, ck0, cv0):
    state_spec = pl.BlockSpec((None, GLA_HEADS, GLA_DK, GLA_DV), lambda b: (b, 0, 0, 0))
    cache_spec = pl.BlockSpec((None, WINDOW, SWA_KV_W), lambda b: (b, 0, 0))
    return pl.pallas_call(
        _mix_sample_kernel,
        grid=(batch,),
        in_specs=[
            pl.BlockSpec(memory_space=pltpu.SMEM),
            pl.BlockSpec((seq, D_MODEL), lambda b: (b, 0)),
        ] + _mix_weight_specs(win, wgu, wbg, wbs, wout) + [state_spec, cache_spec, cache_spec],
        out_specs=[pl.BlockSpec((seq, D_MODEL), lambda b: (b, 0)), state_spec, cache_spec, cache_spec],
        out_shape=_mix_out_shapes(batch, seq),
        scratch_shapes=_mix_scratch(SWA_KEYS)
        + [pltpu.VMEM((seq, D_MODEL), BF16)]
        + [pltpu.VMEM((seq, w), dt) for _, w, dt, _ in _PROJ_LAYOUT],
        compiler_params=pltpu.CompilerParams(
            dimension_semantics=("arbitrary",), vmem_limit_bytes=VMEM_LIMIT),
    )(sinks, x, g, win, wgu, bg, gn, wbg, wbs, wout, state0, ck0, cv0)


def _ffn_weights(norm_g, w_in, w_out):
    return norm_g.reshape(1, D_MODEL), w_in.astype(BF16), w_out.astype(BF16)


def _mix_weights(w_in, w_gate_up):
    o = 0
    parts = {}
    for name, width in (("gq", GLA_QK_W), ("gk", GLA_QK_W), ("gv", GLA_V_W), ("gr", GLA_V_W),
                        ("glr", GLA_RANK), ("sq", SWA_Q_W), ("sk", SWA_KV_W), ("sv", SWA_KV_W),
                        ("gg", D_MODEL), ("gs", D_MODEL)):
        parts[name] = w_in[:, o:o + width]
        o += width
    glr = jnp.pad(parts["glr"], ((0, 0), (0, RANK_PAD - GLA_RANK)))
    win = jnp.concatenate([parts["gq"], parts["gk"], glr, parts["gv"], parts["gr"], parts["sq"],
                           parts["sk"], parts["sv"], parts["gg"], parts["gs"]], axis=1).astype(BF16)
    wgu = jnp.pad(w_gate_up, ((0, RANK_PAD - GLA_RANK), (0, 0))).astype(BF16)
    return win, wgu


def _layer(x, ffn1, mixw, ffn2, gf, state0=None, ck0=None, cv0=None):
    batch, seq, _ = x.shape
    x1 = _ffn(x.reshape(batch * seq, D_MODEL), *ffn1, gf, final_norm=False)
    if state0 is None:
        x2, state, ck, cv = _mix_prompt(x1, batch, seq, *mixw)
    else:
        x2, state, ck, cv = _mix_sample(x1, batch, seq, *mixw, state0, ck0, cv0)
    y = _ffn(x2, *ffn2, gf, final_norm=True)
    return (y.reshape(batch, seq, D_MODEL), state[None],
            ck.reshape(1, batch, WINDOW, SWA_KV_HEADS, SWA_HD),
            cv.reshape(1, batch, WINDOW, SWA_KV_HEADS, SWA_HD))


def kernel(x_prompt, x_sample, state_gla, cache_swa_k, cache_swa_v, norm_ffn1, w_ffn1_in, w_ffn1_out, norm_mix, w_in, w_gla_gate_up, b_gla_gate, gla_norm, swa_sinks, w_branch_gla, w_branch_swa, w_out, norm_ffn2, w_ffn2_in, w_ffn2_out, norm_final):
    ffn1 = _ffn_weights(norm_ffn1[0], w_ffn1_in[0], w_ffn1_out[0])
    ffn2 = _ffn_weights(norm_ffn2[0], w_ffn2_in[0], w_ffn2_out[0])
    win, wgu = _mix_weights(w_in[0], w_gla_gate_up[0])
    mixw = (swa_sinks[0], norm_mix[0].reshape(1, D_MODEL), win, wgu, b_gla_gate[0].reshape(1, GLA_QK_W),
            gla_norm[0].reshape(1, GLA_DV), w_branch_gla[0].astype(BF16), w_branch_swa[0].astype(BF16),
            w_out[0].astype(BF16))
    gf = norm_final.reshape(1, D_MODEL)
    dec_batch = x_sample.shape[0]
    yp, sg_p, sk_p, sv_p = _layer(x_prompt, ffn1, mixw, ffn2, gf)
    ys, sg_s, sk_s, sv_s = _layer(
        x_sample, ffn1, mixw, ffn2, gf, state_gla[0],
        cache_swa_k[0].reshape(dec_batch, WINDOW, SWA_KV_W), cache_swa_v[0].reshape(dec_batch, WINDOW, SWA_KV_W))
    return yp, ys, sg_p, sk_p, sv_p, sg_s, sk_s, sv_s
```

```python
import functools

import jax
import jax.numpy as jnp
from jax import lax
from jax.experimental import pallas as pl
from jax.experimental.pallas import tpu as pltpu

F32 = jnp.float32
BF16 = jnp.bfloat16

D_MODEL = 1024
D_FF = 2816
CHUNK = 64
CHUNK_SHIFT = CHUNK.bit_length() - 1
EPS = 1e-6
LOG2E = 1.4426950408889634
GLA_HEADS = 4
GLA_DK = 128
GLA_DV = 256
GLA_RANK = 16
GLA_TAU = 16.0
SWA_HEADS = 16
SWA_KV_HEADS = 2
SWA_HD = 64
SWA_GROUP = SWA_HEADS // SWA_KV_HEADS
WINDOW = 128
GLA_QK_W = GLA_HEADS * GLA_DK
GLA_V_W = GLA_HEADS * GLA_DV
SWA_Q_W = SWA_HEADS * SWA_HD
SWA_KV_W = SWA_KV_HEADS * SWA_HD

LANES = 128
RANK_PAD = LANES
FF_CHUNK = 256
N_FF_CHUNKS = D_FF // FF_CHUNK
VMEM_LIMIT = 56 * 1024 * 1024

FFN_ROWS = 512
MIX_TILE = 256
GLA_BLOCK = 128
SWA_Q_ROWS = 2 * CHUNK
SWA_KEYS = 2 * WINDOW

IN_HEAD_W = 2 * GLA_QK_W + 2 * GLA_V_W
IN_TAIL_W = SWA_Q_W + 2 * SWA_KV_W + 2 * D_MODEL
_W_HEAD, _W_RANK, _W_TAIL = range(3)
_C_GQ = 0
_C_GK = _C_GQ + GLA_QK_W
_C_GLR = _C_GK + GLA_QK_W

_PROJ_LAYOUT = (
    (F32, None, ((_W_HEAD, 0, 2 * GLA_QK_W), (_W_RANK, 0, RANK_PAD))),
    (BF16, None, ((_W_HEAD, 2 * GLA_QK_W, GLA_V_W),)),
    (F32, None, ((_W_HEAD, 2 * GLA_QK_W + GLA_V_W, GLA_V_W),)),
    (BF16, LOG2E * SWA_HD ** -0.5, ((_W_TAIL, 0, SWA_Q_W),)),
    (F32, None, ((_W_TAIL, SWA_Q_W, 2 * SWA_KV_W),)),
    (F32, None, ((_W_TAIL, SWA_Q_W + 2 * SWA_KV_W, D_MODEL),)),
    (F32, None, ((_W_TAIL, SWA_Q_W + 2 * SWA_KV_W + D_MODEL, D_MODEL),)),
)
_PROJ_SHAPES = tuple((sum(w for _, _, w in segs), dt) for dt, _, segs in _PROJ_LAYOUT)
PROJ_TASK_COLS = 256


def _rmsnorm(x, g):
    ms = jnp.mean(x * x, axis=-1, keepdims=True)
    return x * lax.rsqrt(ms + EPS) * g


def _dot(a, b):
    return jnp.dot(a, b, preferred_element_type=F32)


def _dot_nt(a, b):
    return lax.dot_general(a, b, (((1,), (1,)), ((), ())), preferred_element_type=F32)


def _dot_tn(a, b):
    return lax.dot_general(a, b, (((0,), (0,)), ((), ())), preferred_element_type=F32)


def _ffn_rows(x_ref, o_ref, g_ref, w1_ref, w2_ref, gf_ref, h_ref, acc_ref, final_norm):
    x = x_ref[...]
    h_ref[...] = _rmsnorm(x, g_ref[...]).astype(BF16)
    for c in range(N_FF_CHUNKS):
        cols = slice(c * FF_CHUNK, (c + 1) * FF_CHUNK)
        up_cols = slice(D_FF + c * FF_CHUNK, D_FF + (c + 1) * FF_CHUNK)
        h = h_ref[...]
        gate = _dot(h, w1_ref[:, cols])
        up = _dot(h, w1_ref[:, up_cols])
        act = (gate * jax.nn.sigmoid(gate) * up).astype(BF16)
        down = _dot(act, w2_ref[cols, :])
        if c == 0:
            acc_ref[...] = down
        else:
            acc_ref[...] += down
    y = x + 0.5 * acc_ref[...]
    if final_norm:
        y = _rmsnorm(y, gf_ref[...])
    o_ref[...] = y


def _ffn_kernel(xp_ref, xs_ref, g_ref, w1_ref, w2_ref, gf_ref, yp_ref, ys_ref, h_ref, acc_ref, *,
                final_norm, prompt_steps):
    step = pl.program_id(0)
    rows_s = xs_ref.shape[0]

    @pl.when(step < prompt_steps)
    def _():
        _ffn_rows(xp_ref, yp_ref, g_ref, w1_ref, w2_ref, gf_ref, h_ref, acc_ref, final_norm)

    @pl.when(step == prompt_steps)
    def _():
        _ffn_rows(xs_ref, ys_ref, g_ref, w1_ref, w2_ref, gf_ref,
                  h_ref.at[0:rows_s], acc_ref.at[0:rows_s], final_norm)


def _resident(shape):
    nd = len(shape)
    return pl.BlockSpec(shape, lambda *_: (0,) * nd, pipeline_mode=pl.Buffered(1))


def _ffn(xp, xs, g, w1, w2, gf, *, final_norm):
    n_p, n_s = xp.shape[0], xs.shape[0]
    assert n_p % FFN_ROWS == 0 and n_s <= FFN_ROWS
    prompt_steps = n_p // FFN_ROWS
    prompt_spec = pl.BlockSpec((FFN_ROWS, D_MODEL), lambda i: (jnp.minimum(i, prompt_steps - 1), 0))
    sample_spec = pl.BlockSpec((n_s, D_MODEL), lambda i: (0, 0))
    return pl.pallas_call(
        functools.partial(_ffn_kernel, final_norm=final_norm, prompt_steps=prompt_steps),
        grid=(prompt_steps + 1,),
        in_specs=[
            prompt_spec,
            sample_spec,
            _resident((1, D_MODEL)),
            _resident(w1.shape),
            _resident(w2.shape),
            _resident((1, D_MODEL)),
        ],
        out_specs=[prompt_spec, sample_spec],
        out_shape=[jax.ShapeDtypeStruct((n_p, D_MODEL), F32), jax.ShapeDtypeStruct((n_s, D_MODEL), F32)],
        scratch_shapes=[pltpu.VMEM((FFN_ROWS, D_MODEL), BF16), pltpu.VMEM((FFN_ROWS, D_MODEL), F32)],
        compiler_params=pltpu.CompilerParams(
            dimension_semantics=("arbitrary",), vmem_limit_bytes=VMEM_LIMIT),
    )(xp, xs, g, w1, w2, gf)


def _proj_tasks(h_ref, win_refs, p_refs):
    tasks = []
    for dst, (dtype, scale, segments) in zip(p_refs, _PROJ_LAYOUT):
        d0 = 0
        for part, col0, width in segments:
            for c in range(0, width, PROJ_TASK_COLS):
                w = min(PROJ_TASK_COLS, width - c)

                def task(dst=dst, d=d0 + c, w=w, src=win_refs[part], s=col0 + c, dtype=dtype, scale=scale):
                    v = _dot(h_ref[...], src[:, s:s + w])
                    if scale is not None:
                        v = v * scale
                    dst[:, d:d + w] = v.astype(dtype)

                tasks.append(task)
            d0 += width
    return tasks


class _Interleave:
    def __init__(self, tasks=()):
        self._tasks = list(tasks)

    def __call__(self, n=1):
        for _ in range(n):
            if self._tasks:
                self._tasks.pop(0)()

    def flush(self):
        self(len(self._tasks))


def _ref_rows(p, m, row):
    n_rows, width = p.shape
    if m == 1:
        return jnp.where((row & 1) == 0, p, pltpu.roll(p, 1, 0))
    if m == 2:
        r = row & 3
        return jnp.where(r == 0, pltpu.roll(p, n_rows - 1, 0),
                         jnp.where(r == 1, p,
                                   jnp.where(r == 2, pltpu.roll(p, 1, 0), pltpu.roll(p, 2, 0))))
    nb = n_rows // (2 * m)
    p3 = p.reshape(nb, 2 * m, width)
    r3 = jnp.broadcast_to(p3[:, m - 1:m, :], (nb, 2 * m, width))
    return r3.reshape(n_rows, width)


def _gla_scan(q, k, la, fill):
    n_rows = q.shape[0]
    row = lax.broadcasted_iota(jnp.int32, (n_rows, 1), 0)
    p = la
    zs = []
    m = 1
    while m < n_rows:
        upper = (row & m) != 0
        r = _ref_rows(p, m, row)
        e = jnp.exp2(jnp.where(upper, p, r - p))
        zs.append((jnp.where(upper, q, k) * e).astype(BF16))
        p = p + jnp.where(upper, r, 0.0)
        fill()
        m *= 2
    b = p
    b_last = b[n_rows - 1:n_rows, :]
    return dict(zs=zs, qb=q.astype(BF16), kb=k.astype(BF16),
                q_in=(q * jnp.exp2(b)).astype(BF16), k_out=(k * jnp.exp2(b_last - b)).astype(BF16),
                s_decay=jnp.exp2(b_last))


def _gla_scores(sc, fill):
    n_rows = sc["qb"].shape[0]
    ti = lax.broadcasted_iota(jnp.int32, (n_rows, n_rows), 0)
    si = lax.broadcasted_iota(jnp.int32, (n_rows, n_rows), 1)
    split_level = 31 - lax.clz(jnp.where(ti > si, ti ^ si, 0))
    eye = ti == si
    a_heads = []
    for h in range(GLA_HEADS):
        ks = slice(h * GLA_DK, (h + 1) * GLA_DK)
        a = jnp.where(eye, _dot_nt(sc["qb"][:, ks], sc["kb"][:, ks]), 0.0)
        for li, z in enumerate(sc["zs"]):
            a = jnp.where(split_level == li, _dot_nt(z[:, ks], z[:, ks]), a)
        a_heads.append(a.astype(BF16))
        fill()
    return a_heads


def _gla_out(sc, a_heads, vb, s_ref, sb_ref, gnorm):
    outs = []
    for h in range(GLA_HEADS):
        ks = slice(h * GLA_DK, (h + 1) * GLA_DK)
        vs = slice(h * GLA_DV, (h + 1) * GLA_DV)
        o = _dot(a_heads[h], vb[:, vs]) + _dot(sc["q_in"][:, ks], sb_ref[h])
        dcol = jnp.transpose(jnp.broadcast_to(sc["s_decay"][:, ks], (8, GLA_DK)))[:, 0:1]
        s_new = dcol * s_ref[h] + _dot_tn(sc["k_out"][:, ks], vb[:, vs])
        s_ref[h] = s_new
        sb_ref[h] = s_new.astype(BF16)
        outs.append(_rmsnorm(o, gnorm))
    return jnp.concatenate(outs, axis=1)


def _gla_inputs(qkg, wgu_ref, bg_ref):
    logit = _dot(qkg[:, _C_GLR:_C_GLR + RANK_PAD].astype(BF16), wgu_ref[...]) + bg_ref[...]
    log_alpha = (jnp.minimum(logit, 0.0) - jnp.log1p(jnp.exp(-jnp.abs(logit)))) * (LOG2E / GLA_TAU)
    gq = qkg[:, _C_GQ:_C_GQ + GLA_QK_W] * (GLA_DK ** -0.5)
    gk = qkg[:, _C_GK:_C_GK + GLA_QK_W]
    return gq, gk, log_alpha


def _swa_scores(qb, kn, ks, vn, vs):
    nk = kn.shape[0]
    low = lax.broadcasted_iota(jnp.int32, (nk, LANES), 1) < SWA_HD
    pairs = SWA_GROUP // 2
    res = []
    for g in range(SWA_KV_HEADS):
        k_lo, k_hi = (kn, ks) if g == 0 else (ks, kn)
        v_lo, v_hi = (vn, vs) if g == 0 else (vs, vn)
        kbd = jnp.concatenate([jnp.where(low, k_lo, 0.0), jnp.where(low, 0.0, k_hi)], axis=0).astype(BF16)
        vbd = jnp.concatenate([jnp.where(low, v_lo, 0.0), jnp.where(low, 0.0, v_hi)], axis=0).astype(BF16)
        qs = jnp.concatenate(
            [qb[:, (g * pairs + i) * LANES:(g * pairs + i + 1) * LANES] for i in range(pairs)], axis=0)
        res.append((_dot_nt(qs, kbd), vbd))
    return res


def _swa_softmax(scores, mask, sink_ref, fill):
    tq, nk = mask.shape
    low_o = lax.broadcasted_iota(jnp.int32, (tq, LANES), 1) < SWA_HD
    pairs = SWA_GROUP // 2
    res = []
    for g, (s, _) in enumerate(scores):
        p_rows, inv_rows = [], []
        for i in range(pairs):
            p_half, inv_half = [], []
            for half in range(2):
                sk = sink_ref[g * SWA_GROUP + 2 * i + half] * LOG2E
                sh = jnp.where(mask, s[i * tq:(i + 1) * tq, half * nk:(half + 1) * nk], -jnp.inf)
                mx = jnp.maximum(jnp.max(sh, axis=-1, keepdims=True), sk)
                ph = jnp.exp2(sh - mx)
                den = jnp.sum(ph, axis=-1, keepdims=True) + jnp.exp2(sk - mx)
                p_half.append(ph.astype(BF16))
                inv_half.append(1.0 / den)
            p_rows.append(jnp.concatenate(p_half, axis=1))
            inv_rows.append(jnp.where(low_o, inv_half[0], inv_half[1]))
            fill()
        res.append((jnp.concatenate(p_rows, axis=0), inv_rows))
    return res


def _swa_out(scores, probs):
    pairs = SWA_GROUP // 2
    tiles = []
    for (_, vbd), (p, inv_rows) in zip(scores, probs):
        tq = p.shape[0] // pairs
        o2 = _dot(p, vbd)
        tiles += [o2[i * tq:(i + 1) * tq, :] * inv_rows[i] for i in range(pairs)]
    return jnp.concatenate(tiles, axis=1).astype(BF16)


def _swa_window(kx_ref, vx_ref, r0):
    kx = kx_ref[r0:r0 + SWA_KEYS, :]
    vx = vx_ref[r0:r0 + SWA_KEYS, :]
    return kx, pltpu.roll(kx, SWA_HD, 1), vx, pltpu.roll(vx, SWA_HD, 1)


def _prompt_mask(history_valid):
    qi = lax.broadcasted_iota(jnp.int32, (SWA_Q_ROWS, SWA_KEYS), 0) >> CHUNK_SHIFT
    kj = lax.broadcasted_iota(jnp.int32, (SWA_Q_ROWS, SWA_KEYS), 1)
    kc = kj >> CHUNK_SHIFT
    band = (kc >= qi) & (kc <= qi + WINDOW // CHUNK)
    if history_valid is None:
        return band
    return band & ((kj >= WINDOW) | history_valid)


def _merge(x, o_gla, o_swa, gg, gs, wbg_ref, wbs_ref, wout_ref):
    merged = jax.nn.sigmoid(gg) * _dot(o_gla, wbg_ref[...]) + jax.nn.sigmoid(gs) * _dot(o_swa, wbs_ref[...])
    return x + _dot(merged.astype(BF16), wout_ref[...])


def _mix_tile(p_tile, x, kx_ref, vx_ref, row0, history_valid, refs, fill):
    sink_ref, wgu_ref, bg_ref, gn_ref, wbg_ref, wbs_ref, wout_ref, s_ref, sb_ref = refs
    p_qkg, p_gv, p_gr, p_sq, _, p_gg, p_gs = p_tile
    gq, gk, log_alpha = _gla_inputs(p_qkg[...], wgu_ref, bg_ref)
    gn = gn_ref[...]
    blocks = [slice(c * GLA_BLOCK, (c + 1) * GLA_BLOCK) for c in range(MIX_TILE // GLA_BLOCK)]
    groups = [slice(c * SWA_Q_ROWS, (c + 1) * SWA_Q_ROWS) for c in range(MIX_TILE // SWA_Q_ROWS)]
    assert len(blocks) == len(groups)
    o_gla, o_swa = [], []
    for c, (rs, qs) in enumerate(zip(blocks, groups)):
        sc = _gla_scan(gq[rs], gk[rs], log_alpha[rs], fill)
        scores = _swa_scores(p_sq[qs, :], *_swa_window(kx_ref, vx_ref, row0 + c * SWA_Q_ROWS))
        a_heads = _gla_scores(sc, fill)
        probs = _swa_softmax(scores, _prompt_mask(history_valid if c == 0 else None), sink_ref, fill)
        o_gla.append(_gla_out(sc, a_heads, p_gv[rs, :], s_ref, sb_ref, gn))
        o_swa.append(_swa_out(scores, probs))
    fill.flush()
    gr = p_gr[...]
    og = (jnp.concatenate(o_gla, axis=0) * (gr * jax.nn.sigmoid(gr))).astype(BF16)
    return _merge(x, og, jnp.concatenate(o_swa, axis=0), p_gg[...], p_gs[...], wbg_ref, wbs_ref, wout_ref)


def _mix_prompt_kernel(sink_ref, x_ref, xn_ref, g_ref, wa_ref, wr_ref, wb_ref, wgu_ref, bg_ref, gn_ref, wbg_ref, wbs_ref,
                       wout_ref, y_ref, s_ref, ck_ref, cv_ref, kx_ref, vx_ref, sb_ref, h0_ref, h1_ref, *p_refs,
                       steps_per_seq):
    step = pl.program_id(0)
    seq_step = step % steps_per_seq
    t = MIX_TILE
    n_pieces = len(_PROJ_LAYOUT)
    p_tiles = (p_refs[:n_pieces], p_refs[n_pieces:])
    h_refs = (h0_ref, h1_ref)
    win_refs = (wa_ref, wr_ref, wb_ref)
    refs = (sink_ref, wgu_ref, bg_ref, gn_ref, wbg_ref, wbs_ref, wout_ref, s_ref, sb_ref)

    @pl.when(step == 0)
    def _():
        h0_ref[...] = _rmsnorm(x_ref[0:t, :], g_ref[...]).astype(BF16)
        _Interleave(_proj_tasks(h0_ref, win_refs, p_tiles[0])).flush()

    @pl.when(seq_step == 0)
    def _():
        s_ref[...] = jnp.zeros_like(s_ref)
        sb_ref[...] = jnp.zeros_like(sb_ref)
        ck_ref[...] = jnp.zeros_like(ck_ref)
        cv_ref[...] = jnp.zeros_like(cv_ref)

    kx_ref[0:WINDOW, :] = ck_ref[...]
    vx_ref[0:WINDOW, :] = cv_ref[...]
    for i in range(2):
        p_skv = p_tiles[i][4]
        x_ahead = x_ref[t:2 * t, :] if i == 0 else xn_ref[...]
        h_refs[1 - i][...] = _rmsnorm(x_ahead, g_ref[...]).astype(BF16)
        fill = _Interleave(_proj_tasks(h_refs[1 - i], win_refs, p_tiles[1 - i]))
        kx_ref[WINDOW + i * t:WINDOW + (i + 1) * t, :] = p_skv[:, :SWA_KV_W]
        vx_ref[WINDOW + i * t:WINDOW + (i + 1) * t, :] = p_skv[:, SWA_KV_W:]
        y_ref[i * t:(i + 1) * t, :] = _mix_tile(p_tiles[i], x_ref[i * t:(i + 1) * t, :], kx_ref, vx_ref, i * t,
                                                (seq_step > 0) if i == 0 else None, refs, fill)
    ck_ref[...] = kx_ref[2 * t:2 * t + WINDOW, :]
    cv_ref[...] = vx_ref[2 * t:2 * t + WINDOW, :]


def _mix_sample_kernel(sink_ref, x_ref, g_ref, wa_ref, wr_ref, wb_ref, wgu_ref, bg_ref, gn_ref, wbg_ref, wbs_ref, wout_ref,
                       s0_ref, ck0_ref, cv0_ref, y_ref, s_ref, ck_ref, cv_ref, kx_ref, vx_ref, sb_ref, h_ref,
                       *p_refs, seq):
    x = x_ref[...]
    batch = x.shape[0] // seq
    h_ref[...] = _rmsnorm(x, g_ref[...]).astype(BF16)
    _Interleave(_proj_tasks(h_ref, (wa_ref, wr_ref, wb_ref), p_refs)).flush()
    p_qkg, p_gv, p_gr, p_sq, p_skv, p_gg, p_gs = p_refs
    no_fill = _Interleave()
    gq, gk, log_alpha = _gla_inputs(p_qkg[...], wgu_ref, bg_ref)
    gn = gn_ref[...]
    visible = lax.broadcasted_iota(jnp.int32, (seq, SWA_KEYS), 1) < WINDOW + seq
    pad = jnp.zeros((SWA_KEYS - WINDOW - seq, LANES), F32)
    o_gla, o_swa = [], []
    for b in range(batch):
        rs = slice(b * seq, (b + 1) * seq)
        s_ref[b] = s0_ref[b]
        sb_ref[b] = s0_ref[b].astype(BF16)
        sc = _gla_scan(gq[rs], gk[rs], log_alpha[rs], no_fill)
        o_gla.append(_gla_out(sc, _gla_scores(sc, no_fill), p_gv[rs, :], s_ref.at[b], sb_ref.at[b], gn))

        kx_ref[b, 0:WINDOW, :] = ck0_ref[b]
        vx_ref[b, 0:WINDOW, :] = cv0_ref[b]
        kx_ref[b, WINDOW:WINDOW + seq, :] = p_skv[rs, :SWA_KV_W]
        vx_ref[b, WINDOW:WINDOW + seq, :] = p_skv[rs, SWA_KV_W:]
        kx_ref[b, WINDOW + seq:, :] = pad
        vx_ref[b, WINDOW + seq:, :] = pad
        ck_ref[b] = kx_ref[b, seq:seq + WINDOW, :]
        cv_ref[b] = vx_ref[b, seq:seq + WINDOW, :]
        scores = _swa_scores(p_sq[rs, :], *_swa_window(kx_ref.at[b], vx_ref.at[b], 0))
        o_swa.append(_swa_out(scores, _swa_softmax(scores, visible, sink_ref, no_fill)))
    gr = p_gr[...]
    og = (jnp.concatenate(o_gla, axis=0) * (gr * jax.nn.sigmoid(gr))).astype(BF16)
    y_ref[...] = _merge(x, og, jnp.concatenate(o_swa, axis=0), p_gg[...], p_gs[...], wbg_ref, wbs_ref, wout_ref)


def _mix_weight_specs(win, wgu, wbg, wbs, wout):
    return [_resident((1, D_MODEL))] + [_resident(w.shape) for w in win] + [
        _resident(wgu.shape),
        _resident((1, GLA_QK_W)),
        _resident((1, GLA_DV)),
        _resident(wbg.shape),
        _resident(wbs.shape),
        _resident(wout.shape),
    ]


def _mix_out_shapes(batch, seq):
    return [
        jax.ShapeDtypeStruct((batch * seq, D_MODEL), F32),
        jax.ShapeDtypeStruct((batch, GLA_HEADS, GLA_DK, GLA_DV), F32),
        jax.ShapeDtypeStruct((batch, WINDOW, SWA_KV_W), F32),
        jax.ShapeDtypeStruct((batch, WINDOW, SWA_KV_W), F32),
    ]


def _mix_scratch(key_rows):
    return [pltpu.VMEM((key_rows, LANES), F32), pltpu.VMEM((key_rows, LANES), F32),
            pltpu.VMEM((GLA_HEADS, GLA_DK, GLA_DV), BF16)]


def _mix_prompt(x, batch, seq, sinks, g, win, wgu, bg, gn, wbg, wbs, wout):
    rows = 2 * MIX_TILE
    steps_per_seq = seq // rows
    n_steps = batch * steps_per_seq
    last_tile = batch * seq // MIX_TILE - 1
    state_spec = pl.BlockSpec((None, GLA_HEADS, GLA_DK, GLA_DV), lambda i: (i // steps_per_seq, 0, 0, 0))
    cache_spec = pl.BlockSpec((None, WINDOW, SWA_KV_W), lambda i: (i // steps_per_seq, 0, 0))
    return pl.pallas_call(
        functools.partial(_mix_prompt_kernel, steps_per_seq=steps_per_seq),
        grid=(n_steps,),
        in_specs=[
            pl.BlockSpec(memory_space=pltpu.SMEM),
            pl.BlockSpec((rows, D_MODEL), lambda i: (i, 0)),
            pl.BlockSpec((MIX_TILE, D_MODEL), lambda i: (jnp.minimum(2 * i + 2, last_tile), 0)),
        ] + _mix_weight_specs(win, wgu, wbg, wbs, wout),
        out_specs=[pl.BlockSpec((rows, D_MODEL), lambda i: (i, 0)), state_spec, cache_spec, cache_spec],
        out_shape=_mix_out_shapes(batch, seq),
        scratch_shapes=_mix_scratch(WINDOW + rows)
        + [pltpu.VMEM((MIX_TILE, D_MODEL), BF16)] * 2
        + [pltpu.VMEM((MIX_TILE, w), dt) for w, dt in _PROJ_SHAPES] * 2,
        compiler_params=pltpu.CompilerParams(
            dimension_semantics=("arbitrary",), vmem_limit_bytes=VMEM_LIMIT),
    )(sinks, x, x, g, *win, wgu, bg, gn, wbg, wbs, wout)


def _mix_sample(x, batch, seq, sinks, g, win, wgu, bg, gn, wbg, wbs, wout, state0, ck0, cv0):
    rows = batch * seq
    whole = lambda a: pl.BlockSpec(a.shape, lambda i: (0,) * a.ndim)
    out_shape = _mix_out_shapes(batch, seq)
    return pl.pallas_call(
        functools.partial(_mix_sample_kernel, seq=seq),
        grid=(1,),
        in_specs=[pl.BlockSpec(memory_space=pltpu.SMEM), whole(x)]
        + _mix_weight_specs(win, wgu, wbg, wbs, wout) + [whole(state0), whole(ck0), whole(cv0)],
        out_specs=[whole(o) for o in out_shape],
        out_shape=out_shape,
        scratch_shapes=[pltpu.VMEM((batch, SWA_KEYS, LANES), F32), pltpu.VMEM((batch, SWA_KEYS, LANES), F32),
                        pltpu.VMEM((batch, GLA_HEADS, GLA_DK, GLA_DV), BF16), pltpu.VMEM((rows, D_MODEL), BF16)]
        + [pltpu.VMEM((rows, w), dt) for w, dt in _PROJ_SHAPES],
        compiler_params=pltpu.CompilerParams(
            dimension_semantics=("arbitrary",), vmem_limit_bytes=VMEM_LIMIT),
    )(sinks, x, g, *win, wgu, bg, gn, wbg, wbs, wout, state0, ck0, cv0)


def _ffn_weights(norm_g, w_in, w_out):
    return norm_g.reshape(1, D_MODEL), w_in.astype(BF16), w_out.astype(BF16)


def _mix_weights(w_in, w_gate_up):
    rank0 = IN_HEAD_W
    head = w_in[:, :rank0].astype(BF16)
    rank = jnp.pad(w_in[:, rank0:rank0 + GLA_RANK], ((0, 0), (0, RANK_PAD - GLA_RANK))).astype(BF16)
    tail = w_in[:, rank0 + GLA_RANK:].astype(BF16)
    assert tail.shape[1] == IN_TAIL_W
    wgu = jnp.pad(w_gate_up, ((0, RANK_PAD - GLA_RANK), (0, 0))).astype(BF16)
    return (head, rank, tail), wgu


def _layers(xp, xs, ffn1, mixw, ffn2, gf, state0, ck0, cv0):
    (bp, sp, _), (bs, ss, _) = xp.shape, xs.shape
    x1p, x1s = _ffn(xp.reshape(bp * sp, D_MODEL), xs.reshape(bs * ss, D_MODEL), *ffn1, gf, final_norm=False)
    x2p, *carry_p = _mix_prompt(x1p, bp, sp, *mixw)
    x2s, *carry_s = _mix_sample(x1s, bs, ss, *mixw, state0, ck0, cv0)
    yp, ys = _ffn(x2p, x2s, *ffn2, gf, final_norm=True)

    def carry(batch, state, ck, cv):
        return (state[None], ck.reshape(1, batch, WINDOW, SWA_KV_HEADS, SWA_HD),
                cv.reshape(1, batch, WINDOW, SWA_KV_HEADS, SWA_HD))

    return (yp.reshape(bp, sp, D_MODEL), ys.reshape(bs, ss, D_MODEL), *carry(bp, *carry_p), *carry(bs, *carry_s))


def kernel(x_prompt, x_sample, state_gla, cache_swa_k, cache_swa_v, norm_ffn1, w_ffn1_in, w_ffn1_out, norm_mix, w_in, w_gla_gate_up, b_gla_gate, gla_norm, swa_sinks, w_branch_gla, w_branch_swa, w_out, norm_ffn2, w_ffn2_in, w_ffn2_out, norm_final):
    ffn1 = _ffn_weights(norm_ffn1[0], w_ffn1_in[0], w_ffn1_out[0])
    ffn2 = _ffn_weights(norm_ffn2[0], w_ffn2_in[0], w_ffn2_out[0])
    win, wgu = _mix_weights(w_in[0], w_gla_gate_up[0])
    mixw = (swa_sinks[0], norm_mix[0].reshape(1, D_MODEL), win, wgu, b_gla_gate[0].reshape(1, GLA_QK_W),
            gla_norm[0].reshape(1, GLA_DV), w_branch_gla[0].astype(BF16), w_branch_swa[0].astype(BF16),
            w_out[0].astype(BF16))
    gf = norm_final.reshape(1, D_MODEL)
    dec_batch = x_sample.shape[0]
    return _layers(x_prompt, x_sample, ffn1, mixw, ffn2, gf, state_gla[0],
                   cache_swa_k[0].reshape(dec_batch, WINDOW, SWA_KV_W),
                   cache_swa_v[0].reshape(dec_batch, WINDOW, SWA_KV_W))
```

```python
import functools

import jax
import jax.numpy as jnp
from jax import lax
from jax.experimental import pallas as pl
from jax.experimental.pallas import tpu as pltpu

F32 = jnp.float32
BF16 = jnp.bfloat16

D_MODEL = 1024
D_FF = 2816
CHUNK = 64
CHUNK_SHIFT = CHUNK.bit_length() - 1
EPS = 1e-6
LOG2E = 1.4426950408889634
GLA_HEADS = 4
GLA_DK = 128
GLA_DV = 256
GLA_RANK = 16
GLA_TAU = 16.0
SWA_HEADS = 16
SWA_KV_HEADS = 2
SWA_HD = 64
SWA_GROUP = SWA_HEADS // SWA_KV_HEADS
WINDOW = 128
GLA_QK_W = GLA_HEADS * GLA_DK
GLA_V_W = GLA_HEADS * GLA_DV
SWA_Q_W = SWA_HEADS * SWA_HD
SWA_KV_W = SWA_KV_HEADS * SWA_HD

LANES = 128
RANK_PAD = LANES
FF_CHUNK = 256
N_FF_CHUNKS = D_FF // FF_CHUNK
VMEM_LIMIT = 56 * 1024 * 1024

FFN_ROWS = 512
MIX_TILE = 256
GLA_BLOCK = 128
SWA_Q_ROWS = 2 * CHUNK
SWA_KEYS = 2 * WINDOW

IN_HEAD_W = 2 * GLA_QK_W + 2 * GLA_V_W
IN_TAIL_W = SWA_Q_W + 2 * SWA_KV_W + 2 * D_MODEL
_W_HEAD, _W_RANK, _W_TAIL = range(3)
_C_GQ = 0
_C_GK = _C_GQ + GLA_QK_W
_C_GLR = _C_GK + GLA_QK_W

_PROJ_LAYOUT = (
    (F32, None, ((_W_HEAD, 0, 2 * GLA_QK_W), (_W_RANK, 0, RANK_PAD))),
    (BF16, None, ((_W_HEAD, 2 * GLA_QK_W, GLA_V_W),)),
    (F32, None, ((_W_HEAD, 2 * GLA_QK_W + GLA_V_W, GLA_V_W),)),
    (BF16, LOG2E * SWA_HD ** -0.5, ((_W_TAIL, 0, SWA_Q_W),)),
    (F32, None, ((_W_TAIL, SWA_Q_W, 2 * SWA_KV_W),)),
    (F32, None, ((_W_TAIL, SWA_Q_W + 2 * SWA_KV_W, D_MODEL),)),
    (F32, None, ((_W_TAIL, SWA_Q_W + 2 * SWA_KV_W + D_MODEL, D_MODEL),)),
)
_PROJ_SHAPES = tuple((sum(w for _, _, w in segs), dt) for dt, _, segs in _PROJ_LAYOUT)
PROJ_TASK_COLS = 256


def _rmsnorm(x, g):
    ms = jnp.mean(x * x, axis=-1, keepdims=True)
    return x * lax.rsqrt(ms + EPS) * g


def _dot(a, b):
    return jnp.dot(a, b, preferred_element_type=F32)


def _dot_nt(a, b):
    return lax.dot_general(a, b, (((1,), (1,)), ((), ())), preferred_element_type=F32)


def _dot_tn(a, b):
    return lax.dot_general(a, b, (((0,), (0,)), ((), ())), preferred_element_type=F32)


def _ffn_rows(x_ref, o_ref, g_ref, w1_ref, w2_ref, gf_ref, h_ref, acc_ref, final_norm):
    x = x_ref[...]
    h_ref[...] = _rmsnorm(x, g_ref[...]).astype(BF16)
    for c in range(N_FF_CHUNKS):
        cols = slice(c * FF_CHUNK, (c + 1) * FF_CHUNK)
        up_cols = slice(D_FF + c * FF_CHUNK, D_FF + (c + 1) * FF_CHUNK)
        h = h_ref[...]
        gate = _dot(h, w1_ref[:, cols])
        up = _dot(h, w1_ref[:, up_cols])
        act = (gate * jax.nn.sigmoid(gate) * up).astype(BF16)
        down = _dot(act, w2_ref[cols, :])
        if c == 0:
            acc_ref[...] = down
        else:
            acc_ref[...] += down
    y = x + 0.5 * acc_ref[...]
    if final_norm:
        y = _rmsnorm(y, gf_ref[...])
    o_ref[...] = y


def _ffn_kernel(xp_ref, xs_ref, g_ref, w1_ref, w2_ref, gf_ref, yp_ref, ys_ref, h_ref, acc_ref, *,
                final_norm, prompt_steps):
    step = pl.program_id(0)
    rows_s = xs_ref.shape[0]

    @pl.when(step < prompt_steps)
    def _():
        _ffn_rows(xp_ref, yp_ref, g_ref, w1_ref, w2_ref, gf_ref, h_ref, acc_ref, final_norm)

    @pl.when(step == prompt_steps)
    def _():
        _ffn_rows(xs_ref, ys_ref, g_ref, w1_ref, w2_ref, gf_ref,
                  h_ref.at[0:rows_s], acc_ref.at[0:rows_s], final_norm)


def _resident(shape):
    nd = len(shape)
    return pl.BlockSpec(shape, lambda *_: (0,) * nd, pipeline_mode=pl.Buffered(1))


def _ffn(xp, xs, g, w1, w2, gf, *, final_norm):
    n_p, n_s = xp.shape[0], xs.shape[0]
    assert n_p % FFN_ROWS == 0 and n_s <= FFN_ROWS
    prompt_steps = n_p // FFN_ROWS
    prompt_spec = pl.BlockSpec((FFN_ROWS, D_MODEL), lambda i: (jnp.minimum(i, prompt_steps - 1), 0))
    sample_spec = pl.BlockSpec((n_s, D_MODEL), lambda i: (0, 0))
    return pl.pallas_call(
        functools.partial(_ffn_kernel, final_norm=final_norm, prompt_steps=prompt_steps),
        grid=(prompt_steps + 1,),
        in_specs=[
            prompt_spec,
            sample_spec,
            _resident((1, D_MODEL)),
            _resident(w1.shape),
            _resident(w2.shape),
            _resident((1, D_MODEL)),
        ],
        out_specs=[prompt_spec, sample_spec],
        out_shape=[jax.ShapeDtypeStruct((n_p, D_MODEL), F32), jax.ShapeDtypeStruct((n_s, D_MODEL), F32)],
        scratch_shapes=[pltpu.VMEM((FFN_ROWS, D_MODEL), BF16), pltpu.VMEM((FFN_ROWS, D_MODEL), F32)],
        compiler_params=pltpu.CompilerParams(
            dimension_semantics=("arbitrary",), vmem_limit_bytes=VMEM_LIMIT),
    )(xp, xs, g, w1, w2, gf)


def _proj_tasks(h_ref, win_refs, p_refs):
    tasks = []
    for dst, (dtype, scale, segments) in zip(p_refs, _PROJ_LAYOUT):
        d0 = 0
        for part, col0, width in segments:
            for c in range(0, width, PROJ_TASK_COLS):
                w = min(PROJ_TASK_COLS, width - c)

                def task(dst=dst, d=d0 + c, w=w, src=win_refs[part], s=col0 + c, dtype=dtype, scale=scale):
                    v = _dot(h_ref[...], src[:, s:s + w])
                    if scale is not None:
                        v = v * scale
                    dst[:, d:d + w] = v.astype(dtype)

                tasks.append(task)
            d0 += width
    return tasks


class _Interleave:
    def __init__(self, tasks=()):
        self._tasks = list(tasks)

    def __call__(self, n=1):
        for _ in range(n):
            if self._tasks:
                self._tasks.pop(0)()

    def flush(self):
        self(len(self._tasks))


SUBLANES = 8


def _ref_rows(p, m, row):
    n_rows, width = p.shape
    if 4 * m <= SUBLANES:
        groups = p.reshape(n_rows // SUBLANES, SUBLANES, width)
        back = lambda k: pltpu.roll(groups, k % SUBLANES, 1).reshape(n_rows, width)
        if m == 1:
            return jnp.where((row & 1) == 0, p, back(1))
        r = row & 3
        return jnp.where(r == 0, back(-1), jnp.where(r == 1, p, jnp.where(r == 2, back(1), back(2))))
    nb = n_rows // (2 * m)
    p3 = p.reshape(nb, 2 * m, width)
    r3 = jnp.broadcast_to(p3[:, m - 1:m, :], (nb, 2 * m, width))
    return r3.reshape(n_rows, width)


def _gla_scan(q, k, la, fill):
    n_rows = q.shape[0]
    row = lax.broadcasted_iota(jnp.int32, (n_rows, 1), 0)
    p = la
    zs = []
    m = 1
    while m < n_rows:
        upper = (row & m) != 0
        r = _ref_rows(p, m, row)
        e = jnp.exp2(jnp.where(upper, p, r - p))
        zs.append((jnp.where(upper, q, k) * e).astype(BF16))
        p = p + jnp.where(upper, r, 0.0)
        fill()
        m *= 2
    b = p
    b_last = b[n_rows - 1:n_rows, :]
    return dict(zs=zs, qb=q.astype(BF16), kb=k.astype(BF16),
                q_in=(q * jnp.exp2(b)).astype(BF16), k_out=(k * jnp.exp2(b_last - b)).astype(BF16),
                s_decay=jnp.exp2(b_last))


def _gla_scores(sc, fill):
    n_rows = sc["qb"].shape[0]
    ti = lax.broadcasted_iota(jnp.int32, (n_rows, n_rows), 0)
    si = lax.broadcasted_iota(jnp.int32, (n_rows, n_rows), 1)
    split_level = 31 - lax.clz(jnp.where(ti > si, ti ^ si, 0))
    eye = ti == si
    a_heads = []
    for h in range(GLA_HEADS):
        ks = slice(h * GLA_DK, (h + 1) * GLA_DK)
        a = jnp.where(eye, _dot_nt(sc["qb"][:, ks], sc["kb"][:, ks]), 0.0)
        for li, z in enumerate(sc["zs"]):
            a = jnp.where(split_level == li, _dot_nt(z[:, ks], z[:, ks]), a)
        a_heads.append(a.astype(BF16))
        fill()
    return a_heads


def _gla_out(sc, a_heads, vb, s_ref, sb_ref, gnorm):
    outs = []
    for h in range(GLA_HEADS):
        ks = slice(h * GLA_DK, (h + 1) * GLA_DK)
        vs = slice(h * GLA_DV, (h + 1) * GLA_DV)
        if a_heads[h].shape[1] % LANES == 0:
            o = _dot(jnp.concatenate([a_heads[h], sc["q_in"][:, ks]], axis=1),
                     jnp.concatenate([vb[:, vs], sb_ref[h]], axis=0))
        else:
            o = _dot(a_heads[h], vb[:, vs]) + _dot(sc["q_in"][:, ks], sb_ref[h])
        dcol = jnp.transpose(jnp.broadcast_to(sc["s_decay"][:, ks], (8, GLA_DK)))[:, 0:1]
        s_new = dcol * s_ref[h] + _dot_tn(sc["k_out"][:, ks], vb[:, vs])
        s_ref[h] = s_new
        sb_ref[h] = s_new.astype(BF16)
        outs.append(_rmsnorm(o, gnorm))
    return jnp.concatenate(outs, axis=1)


def _gla_inputs(qkg, wgu_ref, bg_ref):
    logit = _dot(qkg[:, _C_GLR:_C_GLR + RANK_PAD].astype(BF16), wgu_ref[...]) + bg_ref[...]
    log_alpha = (jnp.minimum(logit, 0.0) - jnp.log1p(jnp.exp(-jnp.abs(logit)))) * (LOG2E / GLA_TAU)
    gq = qkg[:, _C_GQ:_C_GQ + GLA_QK_W] * (GLA_DK ** -0.5)
    gk = qkg[:, _C_GK:_C_GK + GLA_QK_W]
    return gq, gk, log_alpha


def _swa_scores(qb, kn, ks, vn, vs):
    nk = kn.shape[0]
    low = lax.broadcasted_iota(jnp.int32, (nk, LANES), 1) < SWA_HD
    pairs = SWA_GROUP // 2
    res = []
    for g in range(SWA_KV_HEADS):
        k_lo, k_hi = (kn, ks) if g == 0 else (ks, kn)
        v_lo, v_hi = (vn, vs) if g == 0 else (vs, vn)
        kbd = jnp.concatenate([jnp.where(low, k_lo, 0.0), jnp.where(low, 0.0, k_hi)], axis=0).astype(BF16)
        vbd = jnp.concatenate([jnp.where(low, v_lo, 0.0), jnp.where(low, 0.0, v_hi)], axis=0).astype(BF16)
        qs = jnp.concatenate(
            [qb[:, (g * pairs + i) * LANES:(g * pairs + i + 1) * LANES] for i in range(pairs)], axis=0)
        res.append((_dot_nt(qs, kbd), vbd))
    return res


def _swa_softmax(scores, mask, sink_ref, fill):
    tq, nk = mask.shape
    low_o = lax.broadcasted_iota(jnp.int32, (tq, LANES), 1) < SWA_HD
    pairs = SWA_GROUP // 2
    res = []
    for g, (s, _) in enumerate(scores):
        p_rows, inv_rows = [], []
        for i in range(pairs):
            p_half, inv_half = [], []
            for half in range(2):
                sk = sink_ref[g * SWA_GROUP + 2 * i + half] * LOG2E
                sh = jnp.where(mask, s[i * tq:(i + 1) * tq, half * nk:(half + 1) * nk], -jnp.inf)
                mx = jnp.maximum(jnp.max(sh, axis=-1, keepdims=True), sk)
                ph = jnp.exp2(sh - mx)
                den = jnp.sum(ph, axis=-1, keepdims=True) + jnp.exp2(sk - mx)
                p_half.append(ph.astype(BF16))
                inv_half.append(1.0 / den)
            p_rows.append(jnp.concatenate(p_half, axis=1))
            inv_rows.append(jnp.where(low_o, inv_half[0], inv_half[1]))
            fill()
        res.append((jnp.concatenate(p_rows, axis=0), inv_rows))
    return res


def _swa_out(scores, probs):
    pairs = SWA_GROUP // 2
    tiles = []
    for (_, vbd), (p, inv_rows) in zip(scores, probs):
        tq = p.shape[0] // pairs
        o2 = _dot(p, vbd)
        tiles += [o2[i * tq:(i + 1) * tq, :] * inv_rows[i] for i in range(pairs)]
    return jnp.concatenate(tiles, axis=1).astype(BF16)


def _swa_window(kx_ref, vx_ref, r0):
    kx = kx_ref[r0:r0 + SWA_KEYS, :]
    vx = vx_ref[r0:r0 + SWA_KEYS, :]
    return kx, pltpu.roll(kx, SWA_HD, 1), vx, pltpu.roll(vx, SWA_HD, 1)


def _prompt_mask(history_valid):
    qi = lax.broadcasted_iota(jnp.int32, (SWA_Q_ROWS, SWA_KEYS), 0) >> CHUNK_SHIFT
    kj = lax.broadcasted_iota(jnp.int32, (SWA_Q_ROWS, SWA_KEYS), 1)
    kc = kj >> CHUNK_SHIFT
    band = (kc >= qi) & (kc <= qi + WINDOW // CHUNK)
    if history_valid is None:
        return band
    return band & ((kj >= WINDOW) | history_valid)


def _merge(x, o_gla, o_swa, gg, gs, wbg_ref, wbs_ref, wout_ref):
    merged = jax.nn.sigmoid(gg) * _dot(o_gla, wbg_ref[...]) + jax.nn.sigmoid(gs) * _dot(o_swa, wbs_ref[...])
    return x + _dot(merged.astype(BF16), wout_ref[...])


def _mix_tile(p_tile, x, kx_ref, vx_ref, row0, history_valid, refs, fill):
    sink_ref, wgu_ref, bg_ref, gn_ref, wbg_ref, wbs_ref, wout_ref, s_ref, sb_ref = refs
    p_qkg, p_gv, p_gr, p_sq, _, p_gg, p_gs = p_tile
    gq, gk, log_alpha = _gla_inputs(p_qkg[...], wgu_ref, bg_ref)
    gn = gn_ref[...]
    blocks = [slice(c * GLA_BLOCK, (c + 1) * GLA_BLOCK) for c in range(MIX_TILE // GLA_BLOCK)]
    groups = [slice(c * SWA_Q_ROWS, (c + 1) * SWA_Q_ROWS) for c in range(MIX_TILE // SWA_Q_ROWS)]
    assert len(blocks) == len(groups)
    o_gla, o_swa = [], []
    for c, (rs, qs) in enumerate(zip(blocks, groups)):
        sc = _gla_scan(gq[rs], gk[rs], log_alpha[rs], fill)
        scores = _swa_scores(p_sq[qs, :], *_swa_window(kx_ref, vx_ref, row0 + c * SWA_Q_ROWS))
        a_heads = _gla_scores(sc, fill)
        probs = _swa_softmax(scores, _prompt_mask(history_valid if c == 0 else None), sink_ref, fill)
        o_gla.append(_gla_out(sc, a_heads, p_gv[rs, :], s_ref, sb_ref, gn))
        o_swa.append(_swa_out(scores, probs))
    fill.flush()
    gr = p_gr[...]
    og = (jnp.concatenate(o_gla, axis=0) * (gr * jax.nn.sigmoid(gr))).astype(BF16)
    return _merge(x, og, jnp.concatenate(o_swa, axis=0), p_gg[...], p_gs[...], wbg_ref, wbs_ref, wout_ref)


def _mix_prompt_kernel(sink_ref, x_ref, xn_ref, g_ref, wa_ref, wr_ref, wb_ref, wgu_ref, bg_ref, gn_ref, wbg_ref, wbs_ref,
                       wout_ref, y_ref, s_ref, ck_ref, cv_ref, kx_ref, vx_ref, sb_ref, h0_ref, h1_ref, *p_refs,
                       steps_per_seq):
    step = pl.program_id(0)
    seq_step = step % steps_per_seq
    t = MIX_TILE
    n_pieces = len(_PROJ_LAYOUT)
    p_tiles = (p_refs[:n_pieces], p_refs[n_pieces:])
    h_refs = (h0_ref, h1_ref)
    win_refs = (wa_ref, wr_ref, wb_ref)
    refs = (sink_ref, wgu_ref, bg_ref, gn_ref, wbg_ref, wbs_ref, wout_ref, s_ref, sb_ref)

    @pl.when(step == 0)
    def _():
        h0_ref[...] = _rmsnorm(x_ref[0:t, :], g_ref[...]).astype(BF16)
        _Interleave(_proj_tasks(h0_ref, win_refs, p_tiles[0])).flush()

    @pl.when(seq_step == 0)
    def _():
        s_ref[...] = jnp.zeros_like(s_ref)
        sb_ref[...] = jnp.zeros_like(sb_ref)
        ck_ref[...] = jnp.zeros_like(ck_ref)
        cv_ref[...] = jnp.zeros_like(cv_ref)

    kx_ref[0:WINDOW, :] = ck_ref[...]
    vx_ref[0:WINDOW, :] = cv_ref[...]
    for i in range(2):
        p_skv = p_tiles[i][4]
        x_ahead = x_ref[t:2 * t, :] if i == 0 else xn_ref[...]
        h_refs[1 - i][...] = _rmsnorm(x_ahead, g_ref[...]).astype(BF16)
        fill = _Interleave(_proj_tasks(h_refs[1 - i], win_refs, p_tiles[1 - i]))
        kx_ref[WINDOW + i * t:WINDOW + (i + 1) * t, :] = p_skv[:, :SWA_KV_W]
        vx_ref[WINDOW + i * t:WINDOW + (i + 1) * t, :] = p_skv[:, SWA_KV_W:]
        y_ref[i * t:(i + 1) * t, :] = _mix_tile(p_tiles[i], x_ref[i * t:(i + 1) * t, :], kx_ref, vx_ref, i * t,
                                                (seq_step > 0) if i == 0 else None, refs, fill)
    ck_ref[...] = kx_ref[2 * t:2 * t + WINDOW, :]
    cv_ref[...] = vx_ref[2 * t:2 * t + WINDOW, :]


def _mix_sample_kernel(sink_ref, x_ref, g_ref, wa_ref, wr_ref, wb_ref, wgu_ref, bg_ref, gn_ref, wbg_ref, wbs_ref, wout_ref,
                       s0_ref, ck0_ref, cv0_ref, y_ref, s_ref, ck_ref, cv_ref, kx_ref, vx_ref, sb_ref, h_ref,
                       *p_refs, seq):
    x = x_ref[...]
    batch = x.shape[0] // seq
    h_ref[...] = _rmsnorm(x, g_ref[...]).astype(BF16)
    _Interleave(_proj_tasks(h_ref, (wa_ref, wr_ref, wb_ref), p_refs)).flush()
    p_qkg, p_gv, p_gr, p_sq, p_skv, p_gg, p_gs = p_refs
    no_fill = _Interleave()
    gq, gk, log_alpha = _gla_inputs(p_qkg[...], wgu_ref, bg_ref)
    gn = gn_ref[...]
    visible = lax.broadcasted_iota(jnp.int32, (seq, SWA_KEYS), 1) < WINDOW + seq
    pad = jnp.zeros((SWA_KEYS - WINDOW - seq, LANES), F32)
    o_gla, o_swa = [], []
    for b in range(batch):
        rs = slice(b * seq, (b + 1) * seq)
        s_ref[b] = s0_ref[b]
        sb_ref[b] = s0_ref[b].astype(BF16)
        sc = _gla_scan(gq[rs], gk[rs], log_alpha[rs], no_fill)
        o_gla.append(_gla_out(sc, _gla_scores(sc, no_fill), p_gv[rs, :], s_ref.at[b], sb_ref.at[b], gn))

        kx_ref[b, 0:WINDOW, :] = ck0_ref[b]
        vx_ref[b, 0:WINDOW, :] = cv0_ref[b]
        kx_ref[b, WINDOW:WINDOW + seq, :] = p_skv[rs, :SWA_KV_W]
        vx_ref[b, WINDOW:WINDOW + seq, :] = p_skv[rs, SWA_KV_W:]
        kx_ref[b, WINDOW + seq:, :] = pad
        vx_ref[b, WINDOW + seq:, :] = pad
        ck_ref[b] = kx_ref[b, seq:seq + WINDOW, :]
        cv_ref[b] = vx_ref[b, seq:seq + WINDOW, :]
        scores = _swa_scores(p_sq[rs, :], *_swa_window(kx_ref.at[b], vx_ref.at[b], 0))
        o_swa.append(_swa_out(scores, _swa_softmax(scores, visible, sink_ref, no_fill)))
    gr = p_gr[...]
    og = (jnp.concatenate(o_gla, axis=0) * (gr * jax.nn.sigmoid(gr))).astype(BF16)
    y_ref[...] = _merge(x, og, jnp.concatenate(o_swa, axis=0), p_gg[...], p_gs[...], wbg_ref, wbs_ref, wout_ref)


def _mix_weight_specs(win, wgu, wbg, wbs, wout):
    return [_resident((1, D_MODEL))] + [_resident(w.shape) for w in win] + [
        _resident(wgu.shape),
        _resident((1, GLA_QK_W)),
        _resident((1, GLA_DV)),
        _resident(wbg.shape),
        _resident(wbs.shape),
        _resident(wout.shape),
    ]


def _mix_out_shapes(batch, seq):
    return [
        jax.ShapeDtypeStruct((batch * seq, D_MODEL), F32),
        jax.ShapeDtypeStruct((batch, GLA_HEADS, GLA_DK, GLA_DV), F32),
        jax.ShapeDtypeStruct((batch, WINDOW, SWA_KV_W), F32),
        jax.ShapeDtypeStruct((batch, WINDOW, SWA_KV_W), F32),
    ]


def _mix_scratch(key_rows):
    return [pltpu.VMEM((key_rows, LANES), F32), pltpu.VMEM((key_rows, LANES), F32),
            pltpu.VMEM((GLA_HEADS, GLA_DK, GLA_DV), BF16)]


def _mix_prompt(x, batch, seq, sinks, g, win, wgu, bg, gn, wbg, wbs, wout):
    rows = 2 * MIX_TILE
    steps_per_seq = seq // rows
    n_steps = batch * steps_per_seq
    last_tile = batch * seq // MIX_TILE - 1
    state_spec = pl.BlockSpec((None, GLA_HEADS, GLA_DK, GLA_DV), lambda i: (i // steps_per_seq, 0, 0, 0))
    cache_spec = pl.BlockSpec((None, WINDOW, SWA_KV_W), lambda i: (i // steps_per_seq, 0, 0))
    return pl.pallas_call(
        functools.partial(_mix_prompt_kernel, steps_per_seq=steps_per_seq),
        grid=(n_steps,),
        in_specs=[
            pl.BlockSpec(memory_space=pltpu.SMEM),
            pl.BlockSpec((rows, D_MODEL), lambda i: (i, 0)),
            pl.BlockSpec((MIX_TILE, D_MODEL), lambda i: (jnp.minimum(2 * i + 2, last_tile), 0)),
        ] + _mix_weight_specs(win, wgu, wbg, wbs, wout),
        out_specs=[pl.BlockSpec((rows, D_MODEL), lambda i: (i, 0)), state_spec, cache_spec, cache_spec],
        out_shape=_mix_out_shapes(batch, seq),
        scratch_shapes=_mix_scratch(WINDOW + rows)
        + [pltpu.VMEM((MIX_TILE, D_MODEL), BF16)] * 2
        + [pltpu.VMEM((MIX_TILE, w), dt) for w, dt in _PROJ_SHAPES] * 2,
        compiler_params=pltpu.CompilerParams(
            dimension_semantics=("arbitrary",), vmem_limit_bytes=VMEM_LIMIT),
    )(sinks, x, x, g, *win, wgu, bg, gn, wbg, wbs, wout)


def _mix_sample(x, batch, seq, sinks, g, win, wgu, bg, gn, wbg, wbs, wout, state0, ck0, cv0):
    rows = batch * seq
    whole = lambda a: pl.BlockSpec(a.shape, lambda i: (0,) * a.ndim)
    out_shape = _mix_out_shapes(batch, seq)
    return pl.pallas_call(
        functools.partial(_mix_sample_kernel, seq=seq),
        grid=(1,),
        in_specs=[pl.BlockSpec(memory_space=pltpu.SMEM), whole(x)]
        + _mix_weight_specs(win, wgu, wbg, wbs, wout) + [whole(state0), whole(ck0), whole(cv0)],
        out_specs=[whole(o) for o in out_shape],
        out_shape=out_shape,
        scratch_shapes=[pltpu.VMEM((batch, SWA_KEYS, LANES), F32), pltpu.VMEM((batch, SWA_KEYS, LANES), F32),
                        pltpu.VMEM((batch, GLA_HEADS, GLA_DK, GLA_DV), BF16), pltpu.VMEM((rows, D_MODEL), BF16)]
        + [pltpu.VMEM((rows, w), dt) for w, dt in _PROJ_SHAPES],
        compiler_params=pltpu.CompilerParams(
            dimension_semantics=("arbitrary",), vmem_limit_bytes=VMEM_LIMIT),
    )(sinks, x, g, *win, wgu, bg, gn, wbg, wbs, wout, state0, ck0, cv0)


def _ffn_weights(norm_g, w_in, w_out):
    return norm_g.reshape(1, D_MODEL), w_in.astype(BF16), w_out.astype(BF16)


def _mix_weights(w_in, w_gate_up):
    rank0 = IN_HEAD_W
    head = w_in[:, :rank0].astype(BF16)
    rank = jnp.pad(w_in[:, rank0:rank0 + GLA_RANK], ((0, 0), (0, RANK_PAD - GLA_RANK))).astype(BF16)
    tail = w_in[:, rank0 + GLA_RANK:].astype(BF16)
    assert tail.shape[1] == IN_TAIL_W
    wgu = jnp.pad(w_gate_up, ((0, RANK_PAD - GLA_RANK), (0, 0))).astype(BF16)
    return (head, rank, tail), wgu


def _layers(xp, xs, ffn1, mixw, ffn2, gf, state0, ck0, cv0):
    (bp, sp, _), (bs, ss, _) = xp.shape, xs.shape
    x1p, x1s = _ffn(xp.reshape(bp * sp, D_MODEL), xs.reshape(bs * ss, D_MODEL), *ffn1, gf, final_norm=False)
    x2p, *carry_p = _mix_prompt(x1p, bp, sp, *mixw)
    x2s, *carry_s = _mix_sample(x1s, bs, ss, *mixw, state0, ck0, cv0)
    yp, ys = _ffn(x2p, x2s, *ffn2, gf, final_norm=True)

    def carry(batch, state, ck, cv):
        return (state[None], ck.reshape(1, batch, WINDOW, SWA_KV_HEADS, SWA_HD),
                cv.reshape(1, batch, WINDOW, SWA_KV_HEADS, SWA_HD))

    return (yp.reshape(bp, sp, D_MODEL), ys.reshape(bs, ss, D_MODEL), *carry(bp, *carry_p), *carry(bs, *carry_s))


def kernel(x_prompt, x_sample, state_gla, cache_swa_k, cache_swa_v, norm_ffn1, w_ffn1_in, w_ffn1_out, norm_mix, w_in, w_gla_gate_up, b_gla_gate, gla_norm, swa_sinks, w_branch_gla, w_branch_swa, w_out, norm_ffn2, w_ffn2_in, w_ffn2_out, norm_final):
    ffn1 = _ffn_weights(norm_ffn1[0], w_ffn1_in[0], w_ffn1_out[0])
    ffn2 = _ffn_weights(norm_ffn2[0], w_ffn2_in[0], w_ffn2_out[0])
    win, wgu = _mix_weights(w_in[0], w_gla_gate_up[0])
    mixw = (swa_sinks[0], norm_mix[0].reshape(1, D_MODEL), win, wgu, b_gla_gate[0].reshape(1, GLA_QK_W),
            gla_norm[0].reshape(1, GLA_DV), w_branch_gla[0].astype(BF16), w_branch_swa[0].astype(BF16),
            w_out[0].astype(BF16))
    gf = norm_final.reshape(1, D_MODEL)
    dec_batch = x_sample.shape[0]
    return _layers(x_prompt, x_sample, ffn1, mixw, ffn2, gf, state_gla[0],
                   cache_swa_k[0].reshape(dec_batch, WINDOW, SWA_KV_W),
                   cache_swa_v[0].reshape(dec_batch, WINDOW, SWA_KV_W))
```

```python
import functools

import jax
import jax.numpy as jnp
from jax import lax
from jax.experimental import pallas as pl
from jax.experimental.pallas import tpu as pltpu

F32 = jnp.float32
BF16 = jnp.bfloat16

D_MODEL = 1024
D_FF = 2816
CHUNK = 64
CHUNK_SHIFT = CHUNK.bit_length() - 1
EPS = 1e-6
LOG2E = 1.4426950408889634
GLA_HEADS = 4
GLA_DK = 128
GLA_DV = 256
GLA_RANK = 16
GLA_TAU = 16.0
SWA_HEADS = 16
SWA_KV_HEADS = 2
SWA_HD = 64
SWA_GROUP = SWA_HEADS // SWA_KV_HEADS
WINDOW = 128
GLA_QK_W = GLA_HEADS * GLA_DK
GLA_V_W = GLA_HEADS * GLA_DV
SWA_Q_W = SWA_HEADS * SWA_HD
SWA_KV_W = SWA_KV_HEADS * SWA_HD

LANES = 128
RANK_PAD = LANES
FF_CHUNK = 256
N_FF_CHUNKS = D_FF // FF_CHUNK
VMEM_LIMIT = 56 * 1024 * 1024

FFN_ROWS = 512
MIX_TILE = 256
GLA_BLOCK = 128
SWA_Q_ROWS = 2 * CHUNK
SWA_KEYS = 2 * WINDOW

IN_HEAD_W = 2 * GLA_QK_W + 2 * GLA_V_W
IN_TAIL_W = SWA_Q_W + 2 * SWA_KV_W + 2 * D_MODEL
_W_HEAD, _W_RANK, _W_TAIL = range(3)
_C_GQ = 0
_C_GK = _C_GQ + GLA_QK_W
_C_GLR = _C_GK + GLA_QK_W

_PROJ_LAYOUT = (
    (F32, None, ((_W_HEAD, 0, 2 * GLA_QK_W), (_W_RANK, 0, RANK_PAD))),
    (BF16, None, ((_W_HEAD, 2 * GLA_QK_W, GLA_V_W),)),
    (F32, None, ((_W_HEAD, 2 * GLA_QK_W + GLA_V_W, GLA_V_W),)),
    (BF16, LOG2E * SWA_HD ** -0.5, ((_W_TAIL, 0, SWA_Q_W),)),
    (F32, None, ((_W_TAIL, SWA_Q_W, 2 * SWA_KV_W),)),
    (F32, None, ((_W_TAIL, SWA_Q_W + 2 * SWA_KV_W, D_MODEL),)),
    (F32, None, ((_W_TAIL, SWA_Q_W + 2 * SWA_KV_W + D_MODEL, D_MODEL),)),
)
_PROJ_SHAPES = tuple((sum(w for _, _, w in segs), dt) for dt, _, segs in _PROJ_LAYOUT)
PROJ_TASK_COLS = 256


def _rmsnorm(x, g):
    ms = jnp.mean(x * x, axis=-1, keepdims=True)
    return x * lax.rsqrt(ms + EPS) * g


def _dot(a, b):
    return jnp.dot(a, b, preferred_element_type=F32)


def _dot_nt(a, b):
    return lax.dot_general(a, b, (((1,), (1,)), ((), ())), preferred_element_type=F32)


def _dot_tn(a, b):
    return lax.dot_general(a, b, (((0,), (0,)), ((), ())), preferred_element_type=F32)


def _ffn_rows(x_ref, o_ref, g_ref, w1_ref, w2_ref, gf_ref, h_ref, acc_ref, final_norm):
    x = x_ref[...]
    h_ref[...] = _rmsnorm(x, g_ref[...]).astype(BF16)
    for c in range(N_FF_CHUNKS):
        cols = slice(c * FF_CHUNK, (c + 1) * FF_CHUNK)
        up_cols = slice(D_FF + c * FF_CHUNK, D_FF + (c + 1) * FF_CHUNK)
        h = h_ref[...]
        gate = _dot(h, w1_ref[:, cols])
        up = _dot(h, w1_ref[:, up_cols])
        act = (gate * jax.nn.sigmoid(gate) * up).astype(BF16)
        down = _dot(act, w2_ref[cols, :])
        if c == 0:
            acc_ref[...] = down
        else:
            acc_ref[...] += down
    y = x + 0.5 * acc_ref[...]
    if final_norm:
        y = _rmsnorm(y, gf_ref[...])
    o_ref[...] = y


def _ffn_kernel(xp_ref, xs_ref, g_ref, w1_ref, w2_ref, gf_ref, yp_ref, ys_ref, h_ref, acc_ref, *,
                final_norm, prompt_steps):
    step = pl.program_id(0)
    rows_s = xs_ref.shape[0]

    @pl.when(step < prompt_steps)
    def _():
        _ffn_rows(xp_ref, yp_ref, g_ref, w1_ref, w2_ref, gf_ref, h_ref, acc_ref, final_norm)

    @pl.when(step == prompt_steps)
    def _():
        _ffn_rows(xs_ref, ys_ref, g_ref, w1_ref, w2_ref, gf_ref,
                  h_ref.at[0:rows_s], acc_ref.at[0:rows_s], final_norm)


def _resident(shape):
    nd = len(shape)
    return pl.BlockSpec(shape, lambda *_: (0,) * nd, pipeline_mode=pl.Buffered(1))


def _ffn(xp, xs, g, w1, w2, gf, *, final_norm):
    n_p, n_s = xp.shape[0], xs.shape[0]
    assert n_p % FFN_ROWS == 0 and n_s <= FFN_ROWS
    prompt_steps = n_p // FFN_ROWS
    prompt_spec = pl.BlockSpec((FFN_ROWS, D_MODEL), lambda i: (jnp.minimum(i, prompt_steps - 1), 0))
    sample_spec = pl.BlockSpec((n_s, D_MODEL), lambda i: (0, 0))
    return pl.pallas_call(
        functools.partial(_ffn_kernel, final_norm=final_norm, prompt_steps=prompt_steps),
        grid=(prompt_steps + 1,),
        in_specs=[
            prompt_spec,
            sample_spec,
            _resident((1, D_MODEL)),
            _resident(w1.shape),
            _resident(w2.shape),
            _resident((1, D_MODEL)),
        ],
        out_specs=[prompt_spec, sample_spec],
        out_shape=[jax.ShapeDtypeStruct((n_p, D_MODEL), F32), jax.ShapeDtypeStruct((n_s, D_MODEL), F32)],
        scratch_shapes=[pltpu.VMEM((FFN_ROWS, D_MODEL), BF16), pltpu.VMEM((FFN_ROWS, D_MODEL), F32)],
        compiler_params=pltpu.CompilerParams(
            dimension_semantics=("arbitrary",), vmem_limit_bytes=VMEM_LIMIT),
    )(xp, xs, g, w1, w2, gf)


def _proj_tasks(h_ref, win_refs, p_refs):
    tasks = []
    for dst, (dtype, scale, segments) in zip(p_refs, _PROJ_LAYOUT):
        d0 = 0
        for part, col0, width in segments:
            for c in range(0, width, PROJ_TASK_COLS):
                w = min(PROJ_TASK_COLS, width - c)

                def task(dst=dst, d=d0 + c, w=w, src=win_refs[part], s=col0 + c, dtype=dtype, scale=scale):
                    v = _dot(h_ref[...], src[:, s:s + w])
                    if scale is not None:
                        v = v * scale
                    dst[:, d:d + w] = v.astype(dtype)

                tasks.append(task)
            d0 += width
    return tasks


class _Interleave:
    def __init__(self, tasks=()):
        self._tasks = list(tasks)

    def __call__(self, n=1):
        for _ in range(n):
            if self._tasks:
                self._tasks.pop(0)()

    def flush(self):
        self(len(self._tasks))


SUBLANES = 8


def _ref_rows(p, m, row):
    n_rows, width = p.shape
    if 4 * m <= SUBLANES:
        groups = p.reshape(n_rows // SUBLANES, SUBLANES, width)
        back = lambda k: pltpu.roll(groups, k % SUBLANES, 1).reshape(n_rows, width)
        if m == 1:
            return jnp.where((row & 1) == 0, p, back(1))
        r = row & 3
        return jnp.where(r == 0, back(-1), jnp.where(r == 1, p, jnp.where(r == 2, back(1), back(2))))
    nb = n_rows // (2 * m)
    p3 = p.reshape(nb, 2 * m, width)
    r3 = jnp.broadcast_to(p3[:, m - 1:m, :], (nb, 2 * m, width))
    return r3.reshape(n_rows, width)


def _gla_scan(q, k, la, fill):
    n_rows = q.shape[0]
    row = lax.broadcasted_iota(jnp.int32, (n_rows, 1), 0)
    p = la
    zs = []
    m = 1
    while m < n_rows:
        upper = (row & m) != 0
        r = _ref_rows(p, m, row)
        e = jnp.exp2(jnp.where(upper, p, r - p))
        zs.append((jnp.where(upper, q, k) * e).astype(BF16))
        p = p + jnp.where(upper, r, 0.0)
        fill()
        m *= 2
    b = p
    b_last = b[n_rows - 1:n_rows, :]
    return dict(zs=zs, qb=q.astype(BF16), kb=k.astype(BF16),
                q_in=(q * jnp.exp2(b)).astype(BF16), k_out=(k * jnp.exp2(b_last - b)).astype(BF16),
                s_decay=jnp.exp2(b_last))


def _gla_scores(sc, fill):
    n_rows = sc["qb"].shape[0]
    ti = lax.broadcasted_iota(jnp.int32, (n_rows, n_rows), 0)
    si = lax.broadcasted_iota(jnp.int32, (n_rows, n_rows), 1)
    split_level = 31 - lax.clz(jnp.where(ti > si, ti ^ si, 0))
    eye = ti == si
    a_heads = []
    for h in range(GLA_HEADS):
        ks = slice(h * GLA_DK, (h + 1) * GLA_DK)
        a = jnp.where(eye, _dot_nt(sc["qb"][:, ks], sc["kb"][:, ks]), 0.0)
        for li, z in enumerate(sc["zs"]):
            a = jnp.where(split_level == li, _dot_nt(z[:, ks], z[:, ks]), a)
        a_heads.append(a.astype(BF16))
        fill()
    return a_heads


def _gla_out(sc, a_heads, vb, s_ref, sb_ref, gnorm):
    outs = []
    for h in range(GLA_HEADS):
        ks = slice(h * GLA_DK, (h + 1) * GLA_DK)
        vs = slice(h * GLA_DV, (h + 1) * GLA_DV)
        if a_heads[h].shape[1] % LANES == 0:
            o = _dot(jnp.concatenate([a_heads[h], sc["q_in"][:, ks]], axis=1),
                     jnp.concatenate([vb[:, vs], sb_ref[h]], axis=0))
        else:
            o = _dot(a_heads[h], vb[:, vs]) + _dot(sc["q_in"][:, ks], sb_ref[h])
        dcol = jnp.transpose(jnp.broadcast_to(sc["s_decay"][:, ks], (8, GLA_DK)))[:, 0:1]
        s_new = dcol * s_ref[h] + _dot_tn(sc["k_out"][:, ks], vb[:, vs])
        s_ref[h] = s_new
        sb_ref[h] = s_new.astype(BF16)
        outs.append(_rmsnorm(o, gnorm))
    return jnp.concatenate(outs, axis=1)


def _gla_inputs(qkg, wgu_ref, bg_ref):
    logit = _dot(qkg[:, _C_GLR:_C_GLR + RANK_PAD].astype(BF16), wgu_ref[...]) + bg_ref[...]
    log_alpha = (jnp.minimum(logit, 0.0) - jnp.log1p(jnp.exp(-jnp.abs(logit)))) * (LOG2E / GLA_TAU)
    gq = qkg[:, _C_GQ:_C_GQ + GLA_QK_W] * (GLA_DK ** -0.5)
    gk = qkg[:, _C_GK:_C_GK + GLA_QK_W]
    return gq, gk, log_alpha


def _swa_operands(qb, kn, ks, vn, vs):
    nk = kn.shape[0]
    low = lax.broadcasted_iota(jnp.int32, (nk, LANES), 1) < SWA_HD
    pairs = SWA_GROUP // 2
    res = []
    for g in range(SWA_KV_HEADS):
        k_lo, k_hi = (kn, ks) if g == 0 else (ks, kn)
        v_lo, v_hi = (vn, vs) if g == 0 else (vs, vn)
        kbd = jnp.concatenate([jnp.where(low, k_lo, 0.0), jnp.where(low, 0.0, k_hi)], axis=0).astype(BF16)
        vbd = jnp.concatenate([jnp.where(low, v_lo, 0.0), jnp.where(low, 0.0, v_hi)], axis=0).astype(BF16)
        qs = jnp.concatenate(
            [qb[:, (g * pairs + i) * LANES:(g * pairs + i + 1) * LANES] for i in range(pairs)], axis=0)
        res.append((kbd, qs, vbd))
    return res


def _swa_scores(qb, kn, ks, vn, vs):
    return [(_dot_nt(qs, kbd), vbd) for kbd, qs, vbd in _swa_operands(qb, kn, ks, vn, vs)]


def _swa_softmax(scores, mask, sink_ref, fill):
    tq, nk = mask.shape
    low_o = lax.broadcasted_iota(jnp.int32, (tq, LANES), 1) < SWA_HD
    pairs = SWA_GROUP // 2
    res = []
    for g, (s, _) in enumerate(scores):
        p_rows, inv_rows = [], []
        for i in range(pairs):
            p_half, inv_half = [], []
            for half in range(2):
                sk = sink_ref[g * SWA_GROUP + 2 * i + half] * LOG2E
                sh = jnp.where(mask, s[i * tq:(i + 1) * tq, half * nk:(half + 1) * nk], -jnp.inf)
                mx = jnp.maximum(jnp.max(sh, axis=-1, keepdims=True), sk)
                ph = jnp.exp2(sh - mx)
                den = jnp.sum(ph, axis=-1, keepdims=True) + jnp.exp2(sk - mx)
                p_half.append(ph.astype(BF16))
                inv_half.append(1.0 / den)
            p_rows.append(jnp.concatenate(p_half, axis=1))
            inv_rows.append(jnp.where(low_o, inv_half[0], inv_half[1]))
            fill()
        res.append((jnp.concatenate(p_rows, axis=0), inv_rows))
    return res


def _swa_out(scores, probs):
    pairs = SWA_GROUP // 2
    tiles = []
    for (_, vbd), (p, inv_rows) in zip(scores, probs):
        tq = p.shape[0] // pairs
        o2 = _dot(p, vbd)
        tiles += [o2[i * tq:(i + 1) * tq, :] * inv_rows[i] for i in range(pairs)]
    return jnp.concatenate(tiles, axis=1).astype(BF16)


def _swa_scores_t(qb, kn, ks, vn, vs):
    return [(_dot_nt(kbd, qs), vbd) for kbd, qs, vbd in _swa_operands(qb, kn, ks, vn, vs)]


def _swa_softmax_t(scores_t, mask_t, sink_ref, fill):
    nk, tq = mask_t.shape
    low_o = lax.broadcasted_iota(jnp.int32, (tq, LANES), 1) < SWA_HD
    pairs = SWA_GROUP // 2
    res = []
    for g, (st, _) in enumerate(scores_t):
        p_cols, inv_tiles = [], []
        for i in range(pairs):
            p_half, inv_half = [], []
            for half in range(2):
                sk = sink_ref[g * SWA_GROUP + 2 * i + half] * LOG2E
                sh = jnp.where(mask_t, st[half * nk:(half + 1) * nk, i * tq:(i + 1) * tq], -jnp.inf)
                mx = jnp.maximum(jnp.max(sh, axis=0, keepdims=True), sk)
                ph = jnp.exp2(sh - mx)
                den = jnp.sum(ph, axis=0, keepdims=True) + jnp.exp2(sk - mx)
                p_half.append(ph.astype(BF16))
                inv_row = jnp.broadcast_to(1.0 / den, (SUBLANES, tq))
                inv_half.append(jnp.transpose(inv_row)[:, 0:1])
            p_cols.append(jnp.concatenate(p_half, axis=0))
            inv_tiles.append(jnp.where(low_o, inv_half[0], inv_half[1]))
            fill()
        res.append((jnp.concatenate(p_cols, axis=1), inv_tiles))
    return res


def _swa_out_t(scores_t, probs_t):
    pairs = SWA_GROUP // 2
    tiles = []
    for (_, vbd), (pt, inv_tiles) in zip(scores_t, probs_t):
        tq = pt.shape[1] // pairs
        o2 = _dot_tn(pt, vbd)
        tiles += [o2[i * tq:(i + 1) * tq, :] * inv_tiles[i] for i in range(pairs)]
    return jnp.concatenate(tiles, axis=1).astype(BF16)


def _swa_window(kx_ref, vx_ref, r0):
    kx = kx_ref[r0:r0 + SWA_KEYS, :]
    vx = vx_ref[r0:r0 + SWA_KEYS, :]
    return kx, pltpu.roll(kx, SWA_HD, 1), vx, pltpu.roll(vx, SWA_HD, 1)


def _prompt_mask_t(history_valid):
    kj = lax.broadcasted_iota(jnp.int32, (SWA_KEYS, SWA_Q_ROWS), 0)
    qi = lax.broadcasted_iota(jnp.int32, (SWA_KEYS, SWA_Q_ROWS), 1) >> CHUNK_SHIFT
    kc = kj >> CHUNK_SHIFT
    band = (kc >= qi) & (kc <= qi + WINDOW // CHUNK)
    if history_valid is None:
        return band
    return band & ((kj >= WINDOW) | history_valid)


def _merge(x, o_gla, o_swa, gg, gs, wbg_ref, wbs_ref, wout_ref):
    merged = jax.nn.sigmoid(gg) * _dot(o_gla, wbg_ref[...]) + jax.nn.sigmoid(gs) * _dot(o_swa, wbs_ref[...])
    return x + _dot(merged.astype(BF16), wout_ref[...])


def _mix_tile(p_tile, x, kx_ref, vx_ref, row0, history_valid, refs, fill):
    sink_ref, wgu_ref, bg_ref, gn_ref, wbg_ref, wbs_ref, wout_ref, s_ref, sb_ref = refs
    p_qkg, p_gv, p_gr, p_sq, _, p_gg, p_gs = p_tile
    gq, gk, log_alpha = _gla_inputs(p_qkg[...], wgu_ref, bg_ref)
    gn = gn_ref[...]
    blocks = [slice(c * GLA_BLOCK, (c + 1) * GLA_BLOCK) for c in range(MIX_TILE // GLA_BLOCK)]
    groups = [slice(c * SWA_Q_ROWS, (c + 1) * SWA_Q_ROWS) for c in range(MIX_TILE // SWA_Q_ROWS)]
    assert len(blocks) == len(groups)
    o_gla, o_swa = [], []
    for c, (rs, qs) in enumerate(zip(blocks, groups)):
        sc = _gla_scan(gq[rs], gk[rs], log_alpha[rs], fill)
        scores = _swa_scores_t(p_sq[qs, :], *_swa_window(kx_ref, vx_ref, row0 + c * SWA_Q_ROWS))
        a_heads = _gla_scores(sc, fill)
        probs = _swa_softmax_t(scores, _prompt_mask_t(history_valid if c == 0 else None), sink_ref, fill)
        o_gla.append(_gla_out(sc, a_heads, p_gv[rs, :], s_ref, sb_ref, gn))
        o_swa.append(_swa_out_t(scores, probs))
    fill.flush()
    gr = p_gr[...]
    og = (jnp.concatenate(o_gla, axis=0) * (gr * jax.nn.sigmoid(gr))).astype(BF16)
    return _merge(x, og, jnp.concatenate(o_swa, axis=0), p_gg[...], p_gs[...], wbg_ref, wbs_ref, wout_ref)


def _mix_prompt_kernel(sink_ref, x_ref, xn_ref, g_ref, wa_ref, wr_ref, wb_ref, wgu_ref, bg_ref, gn_ref, wbg_ref, wbs_ref,
                       wout_ref, y_ref, s_ref, ck_ref, cv_ref, kx_ref, vx_ref, sb_ref, h0_ref, h1_ref, *p_refs,
                       steps_per_seq):
    step = pl.program_id(0)
    seq_step = step % steps_per_seq
    t = MIX_TILE
    n_pieces = len(_PROJ_LAYOUT)
    p_tiles = (p_refs[:n_pieces], p_refs[n_pieces:])
    h_refs = (h0_ref, h1_ref)
    win_refs = (wa_ref, wr_ref, wb_ref)
    refs = (sink_ref, wgu_ref, bg_ref, gn_ref, wbg_ref, wbs_ref, wout_ref, s_ref, sb_ref)

    @pl.when(step == 0)
    def _():
        h0_ref[...] = _rmsnorm(x_ref[0:t, :], g_ref[...]).astype(BF16)
        _Interleave(_proj_tasks(h0_ref, win_refs, p_tiles[0])).flush()

    @pl.when(seq_step == 0)
    def _():
        s_ref[...] = jnp.zeros_like(s_ref)
        sb_ref[...] = jnp.zeros_like(sb_ref)
        ck_ref[...] = jnp.zeros_like(ck_ref)
        cv_ref[...] = jnp.zeros_like(cv_ref)

    kx_ref[0:WINDOW, :] = ck_ref[...]
    vx_ref[0:WINDOW, :] = cv_ref[...]
    for i in range(2):
        p_skv = p_tiles[i][4]
        x_ahead = x_ref[t:2 * t, :] if i == 0 else xn_ref[...]
        h_refs[1 - i][...] = _rmsnorm(x_ahead, g_ref[...]).astype(BF16)
        fill = _Interleave(_proj_tasks(h_refs[1 - i], win_refs, p_tiles[1 - i]))
        kx_ref[WINDOW + i * t:WINDOW + (i + 1) * t, :] = p_skv[:, :SWA_KV_W]
        vx_ref[WINDOW + i * t:WINDOW + (i + 1) * t, :] = p_skv[:, SWA_KV_W:]
        y_ref[i * t:(i + 1) * t, :] = _mix_tile(p_tiles[i], x_ref[i * t:(i + 1) * t, :], kx_ref, vx_ref, i * t,
                                                (seq_step > 0) if i == 0 else None, refs, fill)
    ck_ref[...] = kx_ref[2 * t:2 * t + WINDOW, :]
    cv_ref[...] = vx_ref[2 * t:2 * t + WINDOW, :]


def _mix_sample_kernel(sink_ref, x_ref, g_ref, wa_ref, wr_ref, wb_ref, wgu_ref, bg_ref, gn_ref, wbg_ref, wbs_ref, wout_ref,
                       s0_ref, ck0_ref, cv0_ref, y_ref, s_ref, ck_ref, cv_ref, kx_ref, vx_ref, sb_ref, h_ref,
                       *p_refs, seq):
    x = x_ref[...]
    batch = x.shape[0] // seq
    h_ref[...] = _rmsnorm(x, g_ref[...]).astype(BF16)
    _Interleave(_proj_tasks(h_ref, (wa_ref, wr_ref, wb_ref), p_refs)).flush()
    p_qkg, p_gv, p_gr, p_sq, p_skv, p_gg, p_gs = p_refs
    no_fill = _Interleave()
    gq, gk, log_alpha = _gla_inputs(p_qkg[...], wgu_ref, bg_ref)
    gn = gn_ref[...]
    visible = lax.broadcasted_iota(jnp.int32, (seq, SWA_KEYS), 1) < WINDOW + seq
    pad = jnp.zeros((SWA_KEYS - WINDOW - seq, LANES), F32)
    o_gla, o_swa = [], []
    for b in range(batch):
        rs = slice(b * seq, (b + 1) * seq)
        s_ref[b] = s0_ref[b]
        sb_ref[b] = s0_ref[b].astype(BF16)
        sc = _gla_scan(gq[rs], gk[rs], log_alpha[rs], no_fill)
        o_gla.append(_gla_out(sc, _gla_scores(sc, no_fill), p_gv[rs, :], s_ref.at[b], sb_ref.at[b], gn))

        kx_ref[b, 0:WINDOW, :] = ck0_ref[b]
        vx_ref[b, 0:WINDOW, :] = cv0_ref[b]
        kx_ref[b, WINDOW:WINDOW + seq, :] = p_skv[rs, :SWA_KV_W]
        vx_ref[b, WINDOW:WINDOW + seq, :] = p_skv[rs, SWA_KV_W:]
        kx_ref[b, WINDOW + seq:, :] = pad
        vx_ref[b, WINDOW + seq:, :] = pad
        ck_ref[b] = kx_ref[b, seq:seq + WINDOW, :]
        cv_ref[b] = vx_ref[b, seq:seq + WINDOW, :]
        scores = _swa_scores(p_sq[rs, :], *_swa_window(kx_ref.at[b], vx_ref.at[b], 0))
        o_swa.append(_swa_out(scores, _swa_softmax(scores, visible, sink_ref, no_fill)))
    gr = p_gr[...]
    og = (jnp.concatenate(o_gla, axis=0) * (gr * jax.nn.sigmoid(gr))).astype(BF16)
    y_ref[...] = _merge(x, og, jnp.concatenate(o_swa, axis=0), p_gg[...], p_gs[...], wbg_ref, wbs_ref, wout_ref)


def _mix_weight_specs(win, wgu, wbg, wbs, wout):
    return [_resident((1, D_MODEL))] + [_resident(w.shape) for w in win] + [
        _resident(wgu.shape),
        _resident((1, GLA_QK_W)),
        _resident((1, GLA_DV)),
        _resident(wbg.shape),
        _resident(wbs.shape),
        _resident(wout.shape),
    ]


def _mix_out_shapes(batch, seq):
    return [
        jax.ShapeDtypeStruct((batch * seq, D_MODEL), F32),
        jax.ShapeDtypeStruct((batch, GLA_HEADS, GLA_DK, GLA_DV), F32),
        jax.ShapeDtypeStruct((batch, WINDOW, SWA_KV_W), F32),
        jax.ShapeDtypeStruct((batch, WINDOW, SWA_KV_W), F32),
    ]


def _mix_scratch(key_rows):
    return [pltpu.VMEM((key_rows, LANES), F32), pltpu.VMEM((key_rows, LANES), F32),
            pltpu.VMEM((GLA_HEADS, GLA_DK, GLA_DV), BF16)]


def _mix_prompt(x, batch, seq, sinks, g, win, wgu, bg, gn, wbg, wbs, wout):
    rows = 2 * MIX_TILE
    steps_per_seq = seq // rows
    n_steps = batch * steps_per_seq
    last_tile = batch * seq // MIX_TILE - 1
    state_spec = pl.BlockSpec((None, GLA_HEADS, GLA_DK, GLA_DV), lambda i: (i // steps_per_seq, 0, 0, 0))
    cache_spec = pl.BlockSpec((None, WINDOW, SWA_KV_W), lambda i: (i // steps_per_seq, 0, 0))
    return pl.pallas_call(
        functools.partial(_mix_prompt_kernel, steps_per_seq=steps_per_seq),
        grid=(n_steps,),
        in_specs=[
            pl.BlockSpec(memory_space=pltpu.SMEM),
            pl.BlockSpec((rows, D_MODEL), lambda i: (i, 0)),
            pl.BlockSpec((MIX_TILE, D_MODEL), lambda i: (jnp.minimum(2 * i + 2, last_tile), 0)),
        ] + _mix_weight_specs(win, wgu, wbg, wbs, wout),
        out_specs=[pl.BlockSpec((rows, D_MODEL), lambda i: (i, 0)), state_spec, cache_spec, cache_spec],
        out_shape=_mix_out_shapes(batch, seq),
        scratch_shapes=_mix_scratch(WINDOW + rows)
        + [pltpu.VMEM((MIX_TILE, D_MODEL), BF16)] * 2
        + [pltpu.VMEM((MIX_TILE, w), dt) for w, dt in _PROJ_SHAPES] * 2,
        compiler_params=pltpu.CompilerParams(
            dimension_semantics=("arbitrary",), vmem_limit_bytes=VMEM_LIMIT),
    )(sinks, x, x, g, *win, wgu, bg, gn, wbg, wbs, wout)


def _mix_sample(x, batch, seq, sinks, g, win, wgu, bg, gn, wbg, wbs, wout, state0, ck0, cv0):
    rows = batch * seq
    whole = lambda a: pl.BlockSpec(a.shape, lambda i: (0,) * a.ndim)
    out_shape = _mix_out_shapes(batch, seq)
    return pl.pallas_call(
        functools.partial(_mix_sample_kernel, seq=seq),
        grid=(1,),
        in_specs=[pl.BlockSpec(memory_space=pltpu.SMEM), whole(x)]
        + _mix_weight_specs(win, wgu, wbg, wbs, wout) + [whole(state0), whole(ck0), whole(cv0)],
        out_specs=[whole(o) for o in out_shape],
        out_shape=out_shape,
        scratch_shapes=[pltpu.VMEM((batch, SWA_KEYS, LANES), F32), pltpu.VMEM((batch, SWA_KEYS, LANES), F32),
                        pltpu.VMEM((batch, GLA_HEADS, GLA_DK, GLA_DV), BF16), pltpu.VMEM((rows, D_MODEL), BF16)]
        + [pltpu.VMEM((rows, w), dt) for w, dt in _PROJ_SHAPES],
        compiler_params=pltpu.CompilerParams(
            dimension_semantics=("arbitrary",), vmem_limit_bytes=VMEM_LIMIT),
    )(sinks, x, g, *win, wgu, bg, gn, wbg, wbs, wout, state0, ck0, cv0)


def _ffn_weights(norm_g, w_in, w_out):
    return norm_g.reshape(1, D_MODEL), w_in.astype(BF16), w_out.astype(BF16)


def _mix_weights(w_in, w_gate_up):
    rank0 = IN_HEAD_W
    head = w_in[:, :rank0].astype(BF16)
    rank = jnp.pad(w_in[:, rank0:rank0 + GLA_RANK], ((0, 0), (0, RANK_PAD - GLA_RANK))).astype(BF16)
    tail = w_in[:, rank0 + GLA_RANK:].astype(BF16)
    assert tail.shape[1] == IN_TAIL_W
    wgu = jnp.pad(w_gate_up, ((0, RANK_PAD - GLA_RANK), (0, 0))).astype(BF16)
    return (head, rank, tail), wgu


def _layers(xp, xs, ffn1, mixw, ffn2, gf, state0, ck0, cv0):
    (bp, sp, _), (bs, ss, _) = xp.shape, xs.shape
    x1p, x1s = _ffn(xp.reshape(bp * sp, D_MODEL), xs.reshape(bs * ss, D_MODEL), *ffn1, gf, final_norm=False)
    x2p, *carry_p = _mix_prompt(x1p, bp, sp, *mixw)
    x2s, *carry_s = _mix_sample(x1s, bs, ss, *mixw, state0, ck0, cv0)
    yp, ys = _ffn(x2p, x2s, *ffn2, gf, final_norm=True)

    def carry(batch, state, ck, cv):
        return (state[None], ck.reshape(1, batch, WINDOW, SWA_KV_HEADS, SWA_HD),
                cv.reshape(1, batch, WINDOW, SWA_KV_HEADS, SWA_HD))

    return (yp.reshape(bp, sp, D_MODEL), ys.reshape(bs, ss, D_MODEL), *carry(bp, *carry_p), *carry(bs, *carry_s))


def kernel(x_prompt, x_sample, state_gla, cache_swa_k, cache_swa_v, norm_ffn1, w_ffn1_in, w_ffn1_out, norm_mix, w_in, w_gla_gate_up, b_gla_gate, gla_norm, swa_sinks, w_branch_gla, w_branch_swa, w_out, norm_ffn2, w_ffn2_in, w_ffn2_out, norm_final):
    ffn1 = _ffn_weights(norm_ffn1[0], w_ffn1_in[0], w_ffn1_out[0])
    ffn2 = _ffn_weights(norm_ffn2[0], w_ffn2_in[0], w_ffn2_out[0])
    win, wgu = _mix_weights(w_in[0], w_gla_gate_up[0])
    mixw = (swa_sinks[0], norm_mix[0].reshape(1, D_MODEL), win, wgu, b_gla_gate[0].reshape(1, GLA_QK_W),
            gla_norm[0].reshape(1, GLA_DV), w_branch_gla[0].astype(BF16), w_branch_swa[0].astype(BF16),
            w_out[0].astype(BF16))
    gf = norm_final.reshape(1, D_MODEL)
    dec_batch = x_sample.shape[0]
    return _layers(x_prompt, x_sample, ffn1, mixw, ffn2, gf, state_gla[0],
                   cache_swa_k[0].reshape(dec_batch, WINDOW, SWA_KV_W),
                   cache_swa_v[0].reshape(dec_batch, WINDOW, SWA_KV_W))
```

```python
import functools

import jax
import jax.numpy as jnp
from jax import lax
from jax.experimental import pallas as pl
from jax.experimental.pallas import tpu as pltpu

F32 = jnp.float32
BF16 = jnp.bfloat16

D_MODEL = 1024
D_FF = 2816
CHUNK = 64
CHUNK_SHIFT = CHUNK.bit_length() - 1
EPS = 1e-6
LOG2E = 1.4426950408889634
GLA_HEADS = 4
GLA_DK = 128
GLA_DV = 256
GLA_RANK = 16
GLA_TAU = 16.0
SWA_HEADS = 16
SWA_KV_HEADS = 2
SWA_HD = 64
SWA_GROUP = SWA_HEADS // SWA_KV_HEADS
WINDOW = 128
GLA_QK_W = GLA_HEADS * GLA_DK
GLA_V_W = GLA_HEADS * GLA_DV
SWA_Q_W = SWA_HEADS * SWA_HD
SWA_KV_W = SWA_KV_HEADS * SWA_HD

LANES = 128
RANK_PAD = LANES
FF_CHUNK = 256
N_FF_CHUNKS = D_FF // FF_CHUNK
VMEM_LIMIT = 56 * 1024 * 1024

FFN_ROWS = 512
MIX_TILE = 256
GLA_BLOCK = 64
SWA_Q_ROWS = 2 * CHUNK
SWA_KEYS = 2 * WINDOW

IN_HEAD_W = 2 * GLA_QK_W + 2 * GLA_V_W
IN_TAIL_W = SWA_Q_W + 2 * SWA_KV_W + 2 * D_MODEL
_W_HEAD, _W_RANK, _W_TAIL = range(3)
_C_GQ = 0
_C_GK = _C_GQ + GLA_QK_W
_C_GLR = _C_GK + GLA_QK_W

_PROJ_LAYOUT = (
    (F32, None, ((_W_HEAD, 0, 2 * GLA_QK_W), (_W_RANK, 0, RANK_PAD))),
    (BF16, None, ((_W_HEAD, 2 * GLA_QK_W, GLA_V_W),)),
    (F32, None, ((_W_HEAD, 2 * GLA_QK_W + GLA_V_W, GLA_V_W),)),
    (BF16, LOG2E * SWA_HD ** -0.5, ((_W_TAIL, 0, SWA_Q_W),)),
    (F32, None, ((_W_TAIL, SWA_Q_W, 2 * SWA_KV_W),)),
    (F32, None, ((_W_TAIL, SWA_Q_W + 2 * SWA_KV_W, D_MODEL),)),
    (F32, None, ((_W_TAIL, SWA_Q_W + 2 * SWA_KV_W + D_MODEL, D_MODEL),)),
)
_PROJ_SHAPES = tuple((sum(w for _, _, w in segs), dt) for dt, _, segs in _PROJ_LAYOUT)
PROJ_TASK_COLS = 256


def _rmsnorm(x, g):
    ms = jnp.mean(x * x, axis=-1, keepdims=True)
    return x * lax.rsqrt(ms + EPS) * g


def _dot(a, b):
    return jnp.dot(a, b, preferred_element_type=F32)


def _dot_nt(a, b):
    return lax.dot_general(a, b, (((1,), (1,)), ((), ())), preferred_element_type=F32)


def _dot_tn(a, b):
    return lax.dot_general(a, b, (((0,), (0,)), ((), ())), preferred_element_type=F32)


def _ffn_rows(x_ref, o_ref, g_ref, w1_ref, w2_ref, gf_ref, h_ref, acc_ref, final_norm, ahead=None):
    x = x_ref[...]
    n = x.shape[0]
    if ahead is None:
        h_ref[...] = _rmsnorm(x, g_ref[...]).astype(BF16)
    else:
        xa_ref, ha_ref = ahead
        per_chunk = -(-n // (N_FF_CHUNKS * 16)) * 16
    for c in range(N_FF_CHUNKS):
        cols = slice(c * FF_CHUNK, (c + 1) * FF_CHUNK)
        up_cols = slice(D_FF + c * FF_CHUNK, D_FF + (c + 1) * FF_CHUNK)
        h = h_ref[...]
        gate = _dot(h, w1_ref[:, cols])
        up = _dot(h, w1_ref[:, up_cols])
        act = (gate * jax.nn.sigmoid(gate) * up).astype(BF16)
        down = _dot(act, w2_ref[cols, :])
        if c == 0:
            acc_ref[...] = down
        else:
            acc_ref[...] += down
        if ahead is not None:
            rs = slice(min(n, c * per_chunk), min(n, (c + 1) * per_chunk))
            if rs.start < rs.stop:
                ha_ref[rs, :] = _rmsnorm(xa_ref[rs, :], g_ref[...]).astype(BF16)
    y = x + 0.5 * acc_ref[...]
    if final_norm:
        y = _rmsnorm(y, gf_ref[...])
    o_ref[...] = y


def _ffn_kernel(xp_ref, xa_ref, xs_ref, g_ref, w1_ref, w2_ref, gf_ref, yp_ref, ys_ref, h_ref, acc_ref, *,
                final_norm, prompt_steps):
    step = pl.program_id(0)
    rows_s = xs_ref.shape[0]
    slot = step % 2

    @pl.when(step == 0)
    def _():
        h_ref[0] = _rmsnorm(xp_ref[...], g_ref[...]).astype(BF16)

    @pl.when(step < prompt_steps)
    def _():
        _ffn_rows(xp_ref, yp_ref, g_ref, w1_ref, w2_ref, gf_ref, h_ref.at[slot], acc_ref, final_norm,
                  ahead=(xa_ref, h_ref.at[1 - slot]))

    @pl.when(step == prompt_steps)
    def _():
        _ffn_rows(xs_ref, ys_ref, g_ref, w1_ref, w2_ref, gf_ref,
                  h_ref.at[0, 0:rows_s], acc_ref.at[0:rows_s], final_norm)


def _resident(shape):
    nd = len(shape)
    return pl.BlockSpec(shape, lambda *_: (0,) * nd, pipeline_mode=pl.Buffered(1))


def _ffn(xp, xs, g, w1, w2, gf, *, final_norm):
    n_p, n_s = xp.shape[0], xs.shape[0]
    assert n_p % FFN_ROWS == 0 and n_s <= FFN_ROWS
    prompt_steps = n_p // FFN_ROWS
    prompt_spec = pl.BlockSpec((FFN_ROWS, D_MODEL), lambda i: (jnp.minimum(i, prompt_steps - 1), 0))
    sample_spec = pl.BlockSpec((n_s, D_MODEL), lambda i: (0, 0))
    return pl.pallas_call(
        functools.partial(_ffn_kernel, final_norm=final_norm, prompt_steps=prompt_steps),
        grid=(prompt_steps + 1,),
        in_specs=[
            prompt_spec,
            pl.BlockSpec((FFN_ROWS, D_MODEL), lambda i: (jnp.minimum(i + 1, prompt_steps - 1), 0)),
            sample_spec,
            _resident((1, D_MODEL)),
            _resident(w1.shape),
            _resident(w2.shape),
            _resident((1, D_MODEL)),
        ],
        out_specs=[prompt_spec, sample_spec],
        out_shape=[jax.ShapeDtypeStruct((n_p, D_MODEL), F32), jax.ShapeDtypeStruct((n_s, D_MODEL), F32)],
        scratch_shapes=[pltpu.VMEM((2, FFN_ROWS, D_MODEL), BF16), pltpu.VMEM((FFN_ROWS, D_MODEL), F32)],
        compiler_params=pltpu.CompilerParams(
            dimension_semantics=("arbitrary",), vmem_limit_bytes=VMEM_LIMIT),
    )(xp, xp, xs, g, w1, w2, gf)


def _proj_tasks(h_ref, win_refs, p_refs):
    tasks = []
    for dst, (dtype, scale, segments) in zip(p_refs, _PROJ_LAYOUT):
        d0 = 0
        for part, col0, width in segments:
            for c in range(0, width, PROJ_TASK_COLS):
                w = min(PROJ_TASK_COLS, width - c)

                def task(dst=dst, d=d0 + c, w=w, src=win_refs[part], s=col0 + c, dtype=dtype, scale=scale):
                    v = _dot(h_ref[...], src[:, s:s + w])
                    if scale is not None:
                        v = v * scale
                    dst[:, d:d + w] = v.astype(dtype)

                tasks.append(task)
            d0 += width
    return tasks


class _Interleave:
    def __init__(self, tasks=()):
        self._tasks = list(tasks)

    def __call__(self, n=1):
        for _ in range(n):
            if self._tasks:
                self._tasks.pop(0)()

    def flush(self):
        self(len(self._tasks))


SUBLANES = 8


def _ref_rows(p, m, row):
    n_rows, width = p.shape
    if 4 * m <= SUBLANES:
        groups = p.reshape(n_rows // SUBLANES, SUBLANES, width)
        back = lambda k: pltpu.roll(groups, k % SUBLANES, 1).reshape(n_rows, width)
        if m == 1:
            return jnp.where((row & 1) == 0, p, back(1))
        r = row & 3
        return jnp.where(r == 0, back(-1), jnp.where(r == 1, p, jnp.where(r == 2, back(1), back(2))))
    nb = n_rows // (2 * m)
    p3 = p.reshape(nb, 2 * m, width)
    r3 = jnp.broadcast_to(p3[:, m - 1:m, :], (nb, 2 * m, width))
    return r3.reshape(n_rows, width)


def _gla_scan(q, k, la, fill):
    n_rows = q.shape[0]
    row = lax.broadcasted_iota(jnp.int32, (n_rows, 1), 0)
    p = la
    zs = []
    m = 1
    while m < n_rows:
        upper = (row & m) != 0
        r = _ref_rows(p, m, row)
        e = jnp.exp2(jnp.where(upper, p, r - p))
        zs.append((jnp.where(upper, q, k) * e).astype(BF16))
        p = p + jnp.where(upper, r, 0.0)
        fill()
        m *= 2
    b = p
    b_last = b[n_rows - 1:n_rows, :]
    return dict(zs=zs, qb=q.astype(BF16), kb=k.astype(BF16),
                q_in=(q * jnp.exp2(b)).astype(BF16), k_out=(k * jnp.exp2(b_last - b)).astype(BF16),
                s_decay=jnp.exp2(b_last))


def _gla_scores(sc, fill):
    n_rows = sc["qb"].shape[0]
    n_cols = max(n_rows, LANES)
    ti = lax.broadcasted_iota(jnp.int32, (n_rows, n_cols), 0)
    si = lax.broadcasted_iota(jnp.int32, (n_rows, n_cols), 1)
    split_level = 31 - lax.clz(jnp.where(ti > si, ti ^ si, 0))
    eye = ti == si
    pad = jnp.zeros((n_cols - n_rows, GLA_DK), BF16)

    def keys(x):
        return x if n_cols == n_rows else jnp.concatenate([x, pad], axis=0)

    a_heads = []
    for h in range(GLA_HEADS):
        ks = slice(h * GLA_DK, (h + 1) * GLA_DK)
        a = jnp.where(eye, _dot_nt(sc["qb"][:, ks], keys(sc["kb"][:, ks])), 0.0)
        for li, z in enumerate(sc["zs"]):
            a = jnp.where(split_level == li, _dot_nt(z[:, ks], keys(z[:, ks])), a)
        a_heads.append(a.astype(BF16))
        fill()
    return a_heads


def _gla_out(sc, a_heads, vb, s_ref, sb_ref, gnorm):
    n_rows = vb.shape[0]
    pad = jnp.zeros((a_heads[0].shape[1] - n_rows, GLA_DV), BF16)
    outs = []
    for h in range(GLA_HEADS):
        ks = slice(h * GLA_DK, (h + 1) * GLA_DK)
        vs = slice(h * GLA_DV, (h + 1) * GLA_DV)
        rhs = [vb[:, vs], sb_ref[h]] if pad.shape[0] == 0 else [vb[:, vs], pad, sb_ref[h]]
        o = _dot(jnp.concatenate([a_heads[h], sc["q_in"][:, ks]], axis=1), jnp.concatenate(rhs, axis=0))
        dcol = jnp.transpose(jnp.broadcast_to(sc["s_decay"][:, ks], (8, GLA_DK)))[:, 0:1]
        s_new = dcol * s_ref[h] + _dot_tn(sc["k_out"][:, ks], vb[:, vs])
        s_ref[h] = s_new
        sb_ref[h] = s_new.astype(BF16)
        outs.append(_rmsnorm(o, gnorm))
    return jnp.concatenate(outs, axis=1)


def _gla_inputs(qkg, wgu_ref, bg_ref):
    logit = _dot(qkg[:, _C_GLR:_C_GLR + RANK_PAD].astype(BF16), wgu_ref[...]) + bg_ref[...]
    log_alpha = (jnp.minimum(logit, 0.0) - jnp.log1p(jnp.exp(-jnp.abs(logit)))) * (LOG2E / GLA_TAU)
    gq = qkg[:, _C_GQ:_C_GQ + GLA_QK_W] * (GLA_DK ** -0.5)
    gk = qkg[:, _C_GK:_C_GK + GLA_QK_W]
    return gq, gk, log_alpha


def _swa_operands(qb, kn, ks, vn, vs):
    nk = kn.shape[0]
    low = lax.broadcasted_iota(jnp.int32, (nk, LANES), 1) < SWA_HD
    pairs = SWA_GROUP // 2
    res = []
    for g in range(SWA_KV_HEADS):
        k_lo, k_hi = (kn, ks) if g == 0 else (ks, kn)
        v_lo, v_hi = (vn, vs) if g == 0 else (vs, vn)
        kbd = jnp.concatenate([jnp.where(low, k_lo, 0.0), jnp.where(low, 0.0, k_hi)], axis=0).astype(BF16)
        vbd = jnp.concatenate([jnp.where(low, v_lo, 0.0), jnp.where(low, 0.0, v_hi)], axis=0).astype(BF16)
        qs = jnp.concatenate(
            [qb[:, (g * pairs + i) * LANES:(g * pairs + i + 1) * LANES] for i in range(pairs)], axis=0)
        res.append((kbd, qs, vbd))
    return res


def _swa_scores(qb, kn, ks, vn, vs):
    return [(_dot_nt(qs, kbd), vbd) for kbd, qs, vbd in _swa_operands(qb, kn, ks, vn, vs)]


def _swa_softmax(scores, mask, sink_ref, fill):
    tq, nk = mask.shape
    low_o = lax.broadcasted_iota(jnp.int32, (tq, LANES), 1) < SWA_HD
    pairs = SWA_GROUP // 2
    res = []
    for g, (s, _) in enumerate(scores):
        p_rows, inv_rows = [], []
        for i in range(pairs):
            p_half, inv_half = [], []
            for half in range(2):
                sk = sink_ref[g * SWA_GROUP + 2 * i + half] * LOG2E
                sh = jnp.where(mask, s[i * tq:(i + 1) * tq, half * nk:(half + 1) * nk], -jnp.inf)
                mx = jnp.maximum(jnp.max(sh, axis=-1, keepdims=True), sk)
                ph = jnp.exp2(sh - mx)
                den = jnp.sum(ph, axis=-1, keepdims=True) + jnp.exp2(sk - mx)
                p_half.append(ph.astype(BF16))
                inv_half.append(1.0 / den)
            p_rows.append(jnp.concatenate(p_half, axis=1))
            inv_rows.append(jnp.where(low_o, inv_half[0], inv_half[1]))
            fill()
        res.append((jnp.concatenate(p_rows, axis=0), inv_rows))
    return res


def _swa_out(scores, probs):
    pairs = SWA_GROUP // 2
    tiles = []
    for (_, vbd), (p, inv_rows) in zip(scores, probs):
        tq = p.shape[0] // pairs
        o2 = _dot(p, vbd)
        tiles += [o2[i * tq:(i + 1) * tq, :] * inv_rows[i] for i in range(pairs)]
    return jnp.concatenate(tiles, axis=1).astype(BF16)


def _swa_scores_t(qb, kn, ks, vn, vs):
    return [(_dot_nt(kbd, qs), vbd) for kbd, qs, vbd in _swa_operands(qb, kn, ks, vn, vs)]


def _swa_softmax_t(scores_t, mask_t, sink_ref, fill):
    nk, tq = mask_t.shape
    low_o = lax.broadcasted_iota(jnp.int32, (tq, LANES), 1) < SWA_HD
    pairs = SWA_GROUP // 2
    res = []
    for g, (st, _) in enumerate(scores_t):
        p_cols, inv_tiles = [], []
        for i in range(pairs):
            p_half, inv_half = [], []
            for half in range(2):
                sk = sink_ref[g * SWA_GROUP + 2 * i + half] * LOG2E
                sh = jnp.where(mask_t, st[half * nk:(half + 1) * nk, i * tq:(i + 1) * tq], -jnp.inf)
                mx = jnp.maximum(jnp.max(sh, axis=0, keepdims=True), sk)
                ph = jnp.exp2(sh - mx)
                den = jnp.sum(ph, axis=0, keepdims=True) + jnp.exp2(sk - mx)
                p_half.append(ph.astype(BF16))
                inv_row = jnp.broadcast_to(1.0 / den, (SUBLANES, tq))
                inv_half.append(jnp.transpose(inv_row)[:, 0:1])
            p_cols.append(jnp.concatenate(p_half, axis=0))
            inv_tiles.append(jnp.where(low_o, inv_half[0], inv_half[1]))
            fill()
        res.append((jnp.concatenate(p_cols, axis=1), inv_tiles))
    return res


def _swa_out_t(scores_t, probs_t):
    pairs = SWA_GROUP // 2
    tiles = []
    for (_, vbd), (pt, inv_tiles) in zip(scores_t, probs_t):
        tq = pt.shape[1] // pairs
        o2 = _dot_tn(pt, vbd)
        tiles += [o2[i * tq:(i + 1) * tq, :] * inv_tiles[i] for i in range(pairs)]
    return jnp.concatenate(tiles, axis=1).astype(BF16)


def _swa_window(kx_ref, vx_ref, r0):
    kx = kx_ref[r0:r0 + SWA_KEYS, :]
    vx = vx_ref[r0:r0 + SWA_KEYS, :]
    return kx, pltpu.roll(kx, SWA_HD, 1), vx, pltpu.roll(vx, SWA_HD, 1)


def _prompt_mask_t(history_valid):
    kj = lax.broadcasted_iota(jnp.int32, (SWA_KEYS, SWA_Q_ROWS), 0)
    qi = lax.broadcasted_iota(jnp.int32, (SWA_KEYS, SWA_Q_ROWS), 1) >> CHUNK_SHIFT
    kc = kj >> CHUNK_SHIFT
    band = (kc >= qi) & (kc <= qi + WINDOW // CHUNK)
    if history_valid is None:
        return band
    return band & ((kj >= WINDOW) | history_valid)


def _merge(x, o_gla, o_swa, gg, gs, wbg_ref, wbs_ref, wout_ref):
    merged = jax.nn.sigmoid(gg) * _dot(o_gla, wbg_ref[...]) + jax.nn.sigmoid(gs) * _dot(o_swa, wbs_ref[...])
    return x + _dot(merged.astype(BF16), wout_ref[...])


def _mix_tile(p_tile, x, kx_ref, vx_ref, row0, history_valid, refs, fill):
    sink_ref, wgu_ref, bg_ref, gn_ref, wbg_ref, wbs_ref, wout_ref, s_ref, sb_ref = refs
    p_qkg, p_gv, p_gr, p_sq, _, p_gg, p_gs = p_tile
    gq, gk, log_alpha = _gla_inputs(p_qkg[...], wgu_ref, bg_ref)
    gn = gn_ref[...]
    groups = [slice(c * SWA_Q_ROWS, (c + 1) * SWA_Q_ROWS) for c in range(MIX_TILE // SWA_Q_ROWS)]
    per_group = SWA_Q_ROWS // GLA_BLOCK
    o_gla, o_swa = [], []
    for c, qs in enumerate(groups):
        blocks = [slice(qs.start + j * GLA_BLOCK, qs.start + (j + 1) * GLA_BLOCK) for j in range(per_group)]
        scs = [_gla_scan(gq[rs], gk[rs], log_alpha[rs], fill) for rs in blocks]
        scores = _swa_scores_t(p_sq[qs, :], *_swa_window(kx_ref, vx_ref, row0 + c * SWA_Q_ROWS))
        a_heads = [_gla_scores(sc, fill) for sc in scs]
        probs = _swa_softmax_t(scores, _prompt_mask_t(history_valid if c == 0 else None), sink_ref, fill)
        for rs, sc, ah in zip(blocks, scs, a_heads):
            o_gla.append(_gla_out(sc, ah, p_gv[rs, :], s_ref, sb_ref, gn))
        o_swa.append(_swa_out_t(scores, probs))
    fill.flush()
    gr = p_gr[...]
    og = (jnp.concatenate(o_gla, axis=0) * (gr * jax.nn.sigmoid(gr))).astype(BF16)
    return _merge(x, og, jnp.concatenate(o_swa, axis=0), p_gg[...], p_gs[...], wbg_ref, wbs_ref, wout_ref)


def _mix_prompt_kernel(sink_ref, x_ref, xn_ref, g_ref, wa_ref, wr_ref, wb_ref, wgu_ref, bg_ref, gn_ref, wbg_ref, wbs_ref,
                       wout_ref, y_ref, s_ref, ck_ref, cv_ref, kx_ref, vx_ref, sb_ref, h0_ref, h1_ref, *p_refs,
                       steps_per_seq):
    step = pl.program_id(0)
    seq_step = step % steps_per_seq
    t = MIX_TILE
    n_pieces = len(_PROJ_LAYOUT)
    p_tiles = (p_refs[:n_pieces], p_refs[n_pieces:])
    h_refs = (h0_ref, h1_ref)
    win_refs = (wa_ref, wr_ref, wb_ref)
    refs = (sink_ref, wgu_ref, bg_ref, gn_ref, wbg_ref, wbs_ref, wout_ref, s_ref, sb_ref)

    @pl.when(step == 0)
    def _():
        h0_ref[...] = _rmsnorm(x_ref[0:t, :], g_ref[...]).astype(BF16)
        _Interleave(_proj_tasks(h0_ref, win_refs, p_tiles[0])).flush()

    @pl.when(seq_step == 0)
    def _():
        s_ref[...] = jnp.zeros_like(s_ref)
        sb_ref[...] = jnp.zeros_like(sb_ref)
        ck_ref[...] = jnp.zeros_like(ck_ref)
        cv_ref[...] = jnp.zeros_like(cv_ref)

    kx_ref[0:WINDOW, :] = ck_ref[...]
    vx_ref[0:WINDOW, :] = cv_ref[...]
    for i in range(2):
        p_skv = p_tiles[i][4]
        x_ahead = x_ref[t:2 * t, :] if i == 0 else xn_ref[...]
        h_refs[1 - i][...] = _rmsnorm(x_ahead, g_ref[...]).astype(BF16)
        fill = _Interleave(_proj_tasks(h_refs[1 - i], win_refs, p_tiles[1 - i]))
        kx_ref[WINDOW + i * t:WINDOW + (i + 1) * t, :] = p_skv[:, :SWA_KV_W]
        vx_ref[WINDOW + i * t:WINDOW + (i + 1) * t, :] = p_skv[:, SWA_KV_W:]
        y_ref[i * t:(i + 1) * t, :] = _mix_tile(p_tiles[i], x_ref[i * t:(i + 1) * t, :], kx_ref, vx_ref, i * t,
                                                (seq_step > 0) if i == 0 else None, refs, fill)
    ck_ref[...] = kx_ref[2 * t:2 * t + WINDOW, :]
    cv_ref[...] = vx_ref[2 * t:2 * t + WINDOW, :]


def _mix_sample_kernel(sink_ref, x_ref, g_ref, wa_ref, wr_ref, wb_ref, wgu_ref, bg_ref, gn_ref, wbg_ref, wbs_ref, wout_ref,
                       s0_ref, ck0_ref, cv0_ref, y_ref, s_ref, ck_ref, cv_ref, kx_ref, vx_ref, sb_ref, h_ref,
                       *p_refs, seq):
    x = x_ref[...]
    batch = x.shape[0] // seq
    h_ref[...] = _rmsnorm(x, g_ref[...]).astype(BF16)
    _Interleave(_proj_tasks(h_ref, (wa_ref, wr_ref, wb_ref), p_refs)).flush()
    p_qkg, p_gv, p_gr, p_sq, p_skv, p_gg, p_gs = p_refs
    no_fill = _Interleave()
    gq, gk, log_alpha = _gla_inputs(p_qkg[...], wgu_ref, bg_ref)
    gn = gn_ref[...]
    visible = lax.broadcasted_iota(jnp.int32, (seq, SWA_KEYS), 1) < WINDOW + seq
    pad = jnp.zeros((SWA_KEYS - WINDOW - seq, LANES), F32)
    o_gla, o_swa = [], []
    for b in range(batch):
        rs = slice(b * seq, (b + 1) * seq)
        s_ref[b] = s0_ref[b]
        sb_ref[b] = s0_ref[b].astype(BF16)
        sc = _gla_scan(gq[rs], gk[rs], log_alpha[rs], no_fill)
        o_gla.append(_gla_out(sc, _gla_scores(sc, no_fill), p_gv[rs, :], s_ref.at[b], sb_ref.at[b], gn))

        kx_ref[b, 0:WINDOW, :] = ck0_ref[b]
        vx_ref[b, 0:WINDOW, :] = cv0_ref[b]
        kx_ref[b, WINDOW:WINDOW + seq, :] = p_skv[rs, :SWA_KV_W]
        vx_ref[b, WINDOW:WINDOW + seq, :] = p_skv[rs, SWA_KV_W:]
        kx_ref[b, WINDOW + seq:, :] = pad
        vx_ref[b, WINDOW + seq:, :] = pad
        ck_ref[b] = kx_ref[b, seq:seq + WINDOW, :]
        cv_ref[b] = vx_ref[b, seq:seq + WINDOW, :]
        scores = _swa_scores(p_sq[rs, :], *_swa_window(kx_ref.at[b], vx_ref.at[b], 0))
        o_swa.append(_swa_out(scores, _swa_softmax(scores, visible, sink_ref, no_fill)))
    gr = p_gr[...]
    og = (jnp.concatenate(o_gla, axis=0) * (gr * jax.nn.sigmoid(gr))).astype(BF16)
    y_ref[...] = _merge(x, og, jnp.concatenate(o_swa, axis=0), p_gg[...], p_gs[...], wbg_ref, wbs_ref, wout_ref)


def _mix_weight_specs(win, wgu, wbg, wbs, wout):
    return [_resident((1, D_MODEL))] + [_resident(w.shape) for w in win] + [
        _resident(wgu.shape),
        _resident((1, GLA_QK_W)),
        _resident((1, GLA_DV)),
        _resident(wbg.shape),
        _resident(wbs.shape),
        _resident(wout.shape),
    ]


def _mix_out_shapes(batch, seq):
    return [
        jax.ShapeDtypeStruct((batch * seq, D_MODEL), F32),
        jax.ShapeDtypeStruct((batch, GLA_HEADS, GLA_DK, GLA_DV), F32),
        jax.ShapeDtypeStruct((batch, WINDOW, SWA_KV_W), F32),
        jax.ShapeDtypeStruct((batch, WINDOW, SWA_KV_W), F32),
    ]


def _mix_scratch(key_rows):
    return [pltpu.VMEM((key_rows, LANES), F32), pltpu.VMEM((key_rows, LANES), F32),
            pltpu.VMEM((GLA_HEADS, GLA_DK, GLA_DV), BF16)]


def _mix_prompt(x, batch, seq, sinks, g, win, wgu, bg, gn, wbg, wbs, wout):
    rows = 2 * MIX_TILE
    steps_per_seq = seq // rows
    n_steps = batch * steps_per_seq
    last_tile = batch * seq // MIX_TILE - 1
    state_spec = pl.BlockSpec((None, GLA_HEADS, GLA_DK, GLA_DV), lambda i: (i // steps_per_seq, 0, 0, 0))
    cache_spec = pl.BlockSpec((None, WINDOW, SWA_KV_W), lambda i: (i // steps_per_seq, 0, 0))
    return pl.pallas_call(
        functools.partial(_mix_prompt_kernel, steps_per_seq=steps_per_seq),
        grid=(n_steps,),
        in_specs=[
            pl.BlockSpec(memory_space=pltpu.SMEM),
            pl.BlockSpec((rows, D_MODEL), lambda i: (i, 0)),
            pl.BlockSpec((MIX_TILE, D_MODEL), lambda i: (jnp.minimum(2 * i + 2, last_tile), 0)),
        ] + _mix_weight_specs(win, wgu, wbg, wbs, wout),
        out_specs=[pl.BlockSpec((rows, D_MODEL), lambda i: (i, 0)), state_spec, cache_spec, cache_spec],
        out_shape=_mix_out_shapes(batch, seq),
        scratch_shapes=_mix_scratch(WINDOW + rows)
        + [pltpu.VMEM((MIX_TILE, D_MODEL), BF16)] * 2
        + [pltpu.VMEM((MIX_TILE, w), dt) for w, dt in _PROJ_SHAPES] * 2,
        compiler_params=pltpu.CompilerParams(
            dimension_semantics=("arbitrary",), vmem_limit_bytes=VMEM_LIMIT),
    )(sinks, x, x, g, *win, wgu, bg, gn, wbg, wbs, wout)


def _mix_sample(x, batch, seq, sinks, g, win, wgu, bg, gn, wbg, wbs, wout, state0, ck0, cv0):
    rows = batch * seq
    whole = lambda a: pl.BlockSpec(a.shape, lambda i: (0,) * a.ndim)
    out_shape = _mix_out_shapes(batch, seq)
    return pl.pallas_call(
        functools.partial(_mix_sample_kernel, seq=seq),
        grid=(1,),
        in_specs=[pl.BlockSpec(memory_space=pltpu.SMEM), whole(x)]
        + _mix_weight_specs(win, wgu, wbg, wbs, wout) + [whole(state0), whole(ck0), whole(cv0)],
        out_specs=[whole(o) for o in out_shape],
        out_shape=out_shape,
        scratch_shapes=[pltpu.VMEM((batch, SWA_KEYS, LANES), F32), pltpu.VMEM((batch, SWA_KEYS, LANES), F32),
                        pltpu.VMEM((batch, GLA_HEADS, GLA_DK, GLA_DV), BF16), pltpu.VMEM((rows, D_MODEL), BF16)]
        + [pltpu.VMEM((rows, w), dt) for w, dt in _PROJ_SHAPES],
        compiler_params=pltpu.CompilerParams(
            dimension_semantics=("arbitrary",), vmem_limit_bytes=VMEM_LIMIT),
    )(sinks, x, g, *win, wgu, bg, gn, wbg, wbs, wout, state0, ck0, cv0)


def _ffn_weights(norm_g, w_in, w_out):
    return norm_g.reshape(1, D_MODEL), w_in.astype(BF16), w_out.astype(BF16)


def _mix_weights(w_in, w_gate_up):
    rank0 = IN_HEAD_W
    head = w_in[:, :rank0].astype(BF16)
    rank = jnp.pad(w_in[:, rank0:rank0 + GLA_RANK], ((0, 0), (0, RANK_PAD - GLA_RANK))).astype(BF16)
    tail = w_in[:, rank0 + GLA_RANK:].astype(BF16)
    assert tail.shape[1] == IN_TAIL_W
    wgu = jnp.pad(w_gate_up, ((0, RANK_PAD - GLA_RANK), (0, 0))).astype(BF16)
    return (head, rank, tail), wgu


def _layers(xp, xs, ffn1, mixw, ffn2, gf, state0, ck0, cv0):
    (bp, sp, _), (bs, ss, _) = xp.shape, xs.shape
    x1p, x1s = _ffn(xp.reshape(bp * sp, D_MODEL), xs.reshape(bs * ss, D_MODEL), *ffn1, gf, final_norm=False)
    x2p, *carry_p = _mix_prompt(x1p, bp, sp, *mixw)
    x2s, *carry_s = _mix_sample(x1s, bs, ss, *mixw, state0, ck0, cv0)
    yp, ys = _ffn(x2p, x2s, *ffn2, gf, final_norm=True)

    def carry(batch, state, ck, cv):
        return (state[None], ck.reshape(1, batch, WINDOW, SWA_KV_HEADS, SWA_HD),
                cv.reshape(1, batch, WINDOW, SWA_KV_HEADS, SWA_HD))

    return (yp.reshape(bp, sp, D_MODEL), ys.reshape(bs, ss, D_MODEL), *carry(bp, *carry_p), *carry(bs, *carry_s))


def kernel(x_prompt, x_sample, state_gla, cache_swa_k, cache_swa_v, norm_ffn1, w_ffn1_in, w_ffn1_out, norm_mix, w_in, w_gla_gate_up, b_gla_gate, gla_norm, swa_sinks, w_branch_gla, w_branch_swa, w_out, norm_ffn2, w_ffn2_in, w_ffn2_out, norm_final):
    ffn1 = _ffn_weights(norm_ffn1[0], w_ffn1_in[0], w_ffn1_out[0])
    ffn2 = _ffn_weights(norm_ffn2[0], w_ffn2_in[0], w_ffn2_out[0])
    win, wgu = _mix_weights(w_in[0], w_gla_gate_up[0])
    mixw = (swa_sinks[0], norm_mix[0].reshape(1, D_MODEL), win, wgu, b_gla_gate[0].reshape(1, GLA_QK_W),
            gla_norm[0].reshape(1, GLA_DV), w_branch_gla[0].astype(BF16), w_branch_swa[0].astype(BF16),
            w_out[0].astype(BF16))
    gf = norm_final.reshape(1, D_MODEL)
    dec_batch = x_sample.shape[0]
    return _layers(x_prompt, x_sample, ffn1, mixw, ffn2, gf, state_gla[0],
                   cache_swa_k[0].reshape(dec_batch, WINDOW, SWA_KV_W),
                   cache_swa_v[0].reshape(dec_batch, WINDOW, SWA_KV_W))
```

```python
import functools

import jax
import jax.numpy as jnp
from jax import lax
from jax.experimental import pallas as pl
from jax.experimental.pallas import tpu as pltpu

F32 = jnp.float32
BF16 = jnp.bfloat16

D_MODEL = 1024
D_FF = 2816
CHUNK = 64
CHUNK_SHIFT = CHUNK.bit_length() - 1
EPS = 1e-6
LOG2E = 1.4426950408889634
GLA_HEADS = 4
GLA_DK = 128
GLA_DV = 256
GLA_RANK = 16
GLA_TAU = 16.0
SWA_HEADS = 16
SWA_KV_HEADS = 2
SWA_HD = 64
SWA_GROUP = SWA_HEADS // SWA_KV_HEADS
WINDOW = 128
GLA_QK_W = GLA_HEADS * GLA_DK
GLA_V_W = GLA_HEADS * GLA_DV
SWA_Q_W = SWA_HEADS * SWA_HD
SWA_KV_W = SWA_KV_HEADS * SWA_HD

LANES = 128
RANK_PAD = LANES
FF_CHUNK = 256
N_FF_CHUNKS = D_FF // FF_CHUNK
VMEM_LIMIT = 56 * 1024 * 1024

FFN_ROWS = 512
MIX_TILE = 256
GLA_BLOCK = 64
SWA_Q_ROWS = 2 * CHUNK
SWA_KEYS = 2 * WINDOW

IN_HEAD_W = 2 * GLA_QK_W + 2 * GLA_V_W
IN_TAIL_W = SWA_Q_W + 2 * SWA_KV_W + 2 * D_MODEL
_W_HEAD, _W_RANK, _W_TAIL = range(3)
_C_GQ = 0
_C_GK = _C_GQ + GLA_QK_W
_C_GLR = _C_GK + GLA_QK_W

_PROJ_LAYOUT = (
    (F32, None, ((_W_HEAD, 0, 2 * GLA_QK_W), (_W_RANK, 0, RANK_PAD))),
    (BF16, None, ((_W_HEAD, 2 * GLA_QK_W, GLA_V_W),)),
    (F32, None, ((_W_HEAD, 2 * GLA_QK_W + GLA_V_W, GLA_V_W),)),
    (BF16, LOG2E * SWA_HD ** -0.5, ((_W_TAIL, 0, SWA_Q_W),)),
    (F32, None, ((_W_TAIL, SWA_Q_W, 2 * SWA_KV_W),)),
    (F32, None, ((_W_TAIL, SWA_Q_W + 2 * SWA_KV_W, D_MODEL),)),
    (F32, None, ((_W_TAIL, SWA_Q_W + 2 * SWA_KV_W + D_MODEL, D_MODEL),)),
)
_PROJ_SHAPES = tuple((sum(w for _, _, w in segs), dt) for dt, _, segs in _PROJ_LAYOUT)
PROJ_TASK_COLS = 256


def _rmsnorm(x, g):
    ms = jnp.mean(x * x, axis=-1, keepdims=True)
    return x * lax.rsqrt(ms + EPS) * g


def _dot(a, b):
    return jnp.dot(a, b, preferred_element_type=F32)


def _dot_nt(a, b):
    return lax.dot_general(a, b, (((1,), (1,)), ((), ())), preferred_element_type=F32)


def _dot_tn(a, b):
    return lax.dot_general(a, b, (((0,), (0,)), ((), ())), preferred_element_type=F32)


def _ffn_rows(x_ref, o_ref, g_ref, w1_ref, w2_ref, gf_ref, h_ref, acc_ref, final_norm, ahead=None):
    x = x_ref[...]
    n = x.shape[0]
    if ahead is None:
        h_ref[...] = _rmsnorm(x, g_ref[...]).astype(BF16)
    else:
        xa_ref, ha_ref = ahead
        per_chunk = -(-n // (N_FF_CHUNKS * 16)) * 16
    for c in range(N_FF_CHUNKS):
        cols = slice(c * FF_CHUNK, (c + 1) * FF_CHUNK)
        up_cols = slice(D_FF + c * FF_CHUNK, D_FF + (c + 1) * FF_CHUNK)
        h = h_ref[...]
        gate = _dot(h, w1_ref[:, cols])
        up = _dot(h, w1_ref[:, up_cols])
        act = (gate * jax.nn.sigmoid(gate) * up).astype(BF16)
        down = _dot(act, w2_ref[cols, :])
        if c == 0:
            acc_ref[...] = down
        else:
            acc_ref[...] += down
        if ahead is not None:
            rs = slice(min(n, c * per_chunk), min(n, (c + 1) * per_chunk))
            if rs.start < rs.stop:
                ha_ref[rs, :] = _rmsnorm(xa_ref[rs, :], g_ref[...]).astype(BF16)
    y = x + 0.5 * acc_ref[...]
    if final_norm:
        y = _rmsnorm(y, gf_ref[...])
    o_ref[...] = y


def _ffn_kernel(xp_ref, xa_ref, xs_ref, g_ref, w1_ref, w2_ref, gf_ref, yp_ref, ys_ref, h_ref, acc_ref, *,
                final_norm, prompt_steps):
    step = pl.program_id(0)
    rows_s = xs_ref.shape[0]
    slot = step % 2

    @pl.when(step == 0)
    def _():
        h_ref[0] = _rmsnorm(xp_ref[...], g_ref[...]).astype(BF16)

    @pl.when(step < prompt_steps)
    def _():
        _ffn_rows(xp_ref, yp_ref, g_ref, w1_ref, w2_ref, gf_ref, h_ref.at[slot], acc_ref, final_norm,
                  ahead=(xa_ref, h_ref.at[1 - slot]))

    @pl.when(step == prompt_steps)
    def _():
        _ffn_rows(xs_ref, ys_ref, g_ref, w1_ref, w2_ref, gf_ref,
                  h_ref.at[0, 0:rows_s], acc_ref.at[0:rows_s], final_norm)


def _resident(shape):
    nd = len(shape)
    return pl.BlockSpec(shape, lambda *_: (0,) * nd, pipeline_mode=pl.Buffered(1))


def _ffn(xp, xs, g, w1, w2, gf, *, final_norm):
    n_p, n_s = xp.shape[0], xs.shape[0]
    assert n_p % FFN_ROWS == 0 and n_s <= FFN_ROWS
    prompt_steps = n_p // FFN_ROWS
    prompt_spec = pl.BlockSpec((FFN_ROWS, D_MODEL), lambda i: (jnp.minimum(i, prompt_steps - 1), 0))
    sample_spec = pl.BlockSpec((n_s, D_MODEL), lambda i: (0, 0))
    return pl.pallas_call(
        functools.partial(_ffn_kernel, final_norm=final_norm, prompt_steps=prompt_steps),
        grid=(prompt_steps + 1,),
        in_specs=[
            prompt_spec,
            pl.BlockSpec((FFN_ROWS, D_MODEL), lambda i: (jnp.minimum(i + 1, prompt_steps - 1), 0)),
            sample_spec,
            _resident((1, D_MODEL)),
            _resident(w1.shape),
            _resident(w2.shape),
            _resident((1, D_MODEL)),
        ],
        out_specs=[prompt_spec, sample_spec],
        out_shape=[jax.ShapeDtypeStruct((n_p, D_MODEL), F32), jax.ShapeDtypeStruct((n_s, D_MODEL), F32)],
        scratch_shapes=[pltpu.VMEM((2, FFN_ROWS, D_MODEL), BF16), pltpu.VMEM((FFN_ROWS, D_MODEL), F32)],
        compiler_params=pltpu.CompilerParams(
            dimension_semantics=("arbitrary",), vmem_limit_bytes=VMEM_LIMIT),
    )(xp, xp, xs, g, w1, w2, gf)


def _proj_tasks(h_ref, win_refs, p_refs):
    tasks = []
    for dst, (dtype, scale, segments) in zip(p_refs, _PROJ_LAYOUT):
        d0 = 0
        for part, col0, width in segments:
            for c in range(0, width, PROJ_TASK_COLS):
                w = min(PROJ_TASK_COLS, width - c)

                def task(dst=dst, d=d0 + c, w=w, src=win_refs[part], s=col0 + c, dtype=dtype, scale=scale):
                    v = _dot(h_ref[...], src[:, s:s + w])
                    if scale is not None:
                        v = v * scale
                    dst[:, d:d + w] = v.astype(dtype)

                tasks.append(task)
            d0 += width
    return tasks


class _Interleave:
    def __init__(self, tasks=()):
        self._tasks = list(tasks)

    def __call__(self, n=1):
        for _ in range(n):
            if self._tasks:
                self._tasks.pop(0)()

    def flush(self):
        self(len(self._tasks))


SUBLANES = 8


def _ref_rows(p, m, row):
    n_rows, width = p.shape
    if 4 * m <= SUBLANES:
        groups = p.reshape(n_rows // SUBLANES, SUBLANES, width)
        back = lambda k: pltpu.roll(groups, k % SUBLANES, 1).reshape(n_rows, width)
        if m == 1:
            return jnp.where((row & 1) == 0, p, back(1))
        r = row & 3
        return jnp.where(r == 0, back(-1), jnp.where(r == 1, p, jnp.where(r == 2, back(1), back(2))))
    nb = n_rows // (2 * m)
    p3 = p.reshape(nb, 2 * m, width)
    r3 = jnp.broadcast_to(p3[:, m - 1:m, :], (nb, 2 * m, width))
    return r3.reshape(n_rows, width)


def _gla_scan(q, k, la, fill):
    n_rows = q.shape[0]
    row = lax.broadcasted_iota(jnp.int32, (n_rows, 1), 0)
    p = la
    zs = []
    m = 1
    while m < n_rows:
        upper = (row & m) != 0
        r = _ref_rows(p, m, row)
        e = jnp.exp2(jnp.where(upper, p, r - p))
        zs.append((jnp.where(upper, q, k) * e).astype(BF16))
        p = p + jnp.where(upper, r, 0.0)
        fill()
        m *= 2
    b = p
    b_last = b[n_rows - 1:n_rows, :]
    return dict(zs=zs, qb=q.astype(BF16), kb=k.astype(BF16),
                q_in=(q * jnp.exp2(b)).astype(BF16), k_out=(k * jnp.exp2(b_last - b)).astype(BF16),
                s_decay=jnp.exp2(b_last))


def _gla_scores(sc, fill):
    n_rows = sc["qb"].shape[0]
    n_cols = max(n_rows, LANES)
    ti = lax.broadcasted_iota(jnp.int32, (n_rows, n_cols), 0)
    si = lax.broadcasted_iota(jnp.int32, (n_rows, n_cols), 1)
    split_level = 31 - lax.clz(jnp.where(ti > si, ti ^ si, 0))
    eye = ti == si
    pad = jnp.zeros((n_cols - n_rows, GLA_DK), BF16)

    def keys(x):
        return x if n_cols == n_rows else jnp.concatenate([x, pad], axis=0)

    a_heads = []
    for h in range(GLA_HEADS):
        ks = slice(h * GLA_DK, (h + 1) * GLA_DK)
        a = jnp.where(eye, _dot_nt(sc["qb"][:, ks], keys(sc["kb"][:, ks])), 0.0)
        for li, z in enumerate(sc["zs"]):
            a = jnp.where(split_level == li, _dot_nt(z[:, ks], keys(z[:, ks])), a)
        a_heads.append(a.astype(BF16))
        fill()
    return a_heads


def _gla_out(sc, a_heads, vb, s_ref, sb_ref, gnorm):
    n_rows = vb.shape[0]
    pad = jnp.zeros((a_heads[0].shape[1] - n_rows, GLA_DV), BF16)
    outs = []
    for h in range(GLA_HEADS):
        ks = slice(h * GLA_DK, (h + 1) * GLA_DK)
        vs = slice(h * GLA_DV, (h + 1) * GLA_DV)
        rhs = [vb[:, vs], sb_ref[h]] if pad.shape[0] == 0 else [vb[:, vs], pad, sb_ref[h]]
        o = _dot(jnp.concatenate([a_heads[h], sc["q_in"][:, ks]], axis=1), jnp.concatenate(rhs, axis=0))
        dcol = jnp.transpose(jnp.broadcast_to(sc["s_decay"][:, ks], (SUBLANES, GLA_DK)))[:, 0:1]
        s_new = dcol * s_ref[h] + _dot_tn(sc["k_out"][:, ks], vb[:, vs])
        s_ref[h] = s_new
        sb_ref[h] = s_new.astype(BF16)
        outs.append(_rmsnorm(o, gnorm))
    return jnp.concatenate(outs, axis=1)


def _gla_inputs(qkg, wgu_ref, bg_ref):
    logit = _dot(qkg[:, _C_GLR:_C_GLR + RANK_PAD].astype(BF16), wgu_ref[...]) + bg_ref[...]
    log_alpha = (jnp.minimum(logit, 0.0) - jnp.log1p(jnp.exp(-jnp.abs(logit)))) * (LOG2E / GLA_TAU)
    gq = qkg[:, _C_GQ:_C_GQ + GLA_QK_W] * (GLA_DK ** -0.5)
    gk = qkg[:, _C_GK:_C_GK + GLA_QK_W]
    return gq, gk, log_alpha


def _swa_operands(qb, kn, ks, vn, vs):
    nk = kn.shape[0]
    low = lax.broadcasted_iota(jnp.int32, (nk, LANES), 1) < SWA_HD
    pairs = SWA_GROUP // 2
    res = []
    for g in range(SWA_KV_HEADS):
        k_lo, k_hi = (kn, ks) if g == 0 else (ks, kn)
        v_lo, v_hi = (vn, vs) if g == 0 else (vs, vn)
        kbd = jnp.concatenate([jnp.where(low, k_lo, 0.0), jnp.where(low, 0.0, k_hi)], axis=0).astype(BF16)
        vbd = jnp.concatenate([jnp.where(low, v_lo, 0.0), jnp.where(low, 0.0, v_hi)], axis=0).astype(BF16)
        qs = jnp.concatenate(
            [qb[:, (g * pairs + i) * LANES:(g * pairs + i + 1) * LANES] for i in range(pairs)], axis=0)
        res.append((kbd, qs, vbd))
    return res


def _swa_scores(qb, kn, ks, vn, vs):
    return [(_dot_nt(qs, kbd), vbd) for kbd, qs, vbd in _swa_operands(qb, kn, ks, vn, vs)]


def _swa_softmax(scores, mask, sink_ref, fill):
    tq, nk = mask.shape
    low_o = lax.broadcasted_iota(jnp.int32, (tq, LANES), 1) < SWA_HD
    pairs = SWA_GROUP // 2
    res = []
    for g, (s, _) in enumerate(scores):
        p_rows, inv_rows = [], []
        for i in range(pairs):
            p_half, inv_half = [], []
            for half in range(2):
                sk = sink_ref[g * SWA_GROUP + 2 * i + half] * LOG2E
                sh = jnp.where(mask, s[i * tq:(i + 1) * tq, half * nk:(half + 1) * nk], -jnp.inf)
                mx = jnp.maximum(jnp.max(sh, axis=-1, keepdims=True), sk)
                ph = jnp.exp2(sh - mx)
                den = jnp.sum(ph, axis=-1, keepdims=True) + jnp.exp2(sk - mx)
                p_half.append(ph.astype(BF16))
                inv_half.append(1.0 / den)
            p_rows.append(jnp.concatenate(p_half, axis=1))
            inv_rows.append(jnp.where(low_o, inv_half[0], inv_half[1]))
            fill()
        res.append((jnp.concatenate(p_rows, axis=0), inv_rows))
    return res


def _swa_out(scores, probs):
    pairs = SWA_GROUP // 2
    tiles = []
    for (_, vbd), (p, inv_rows) in zip(scores, probs):
        tq = p.shape[0] // pairs
        o2 = _dot(p, vbd)
        tiles += [o2[i * tq:(i + 1) * tq, :] * inv_rows[i] for i in range(pairs)]
    return jnp.concatenate(tiles, axis=1).astype(BF16)


def _swa_scores_t(qb, kn, ks, vn, vs):
    return [(_dot_nt(kbd, qs), vbd) for kbd, qs, vbd in _swa_operands(qb, kn, ks, vn, vs)]


def _swa_softmax_t(scores_t, mask_t, sink_ref, fill):
    nk, tq = mask_t.shape
    low_o = lax.broadcasted_iota(jnp.int32, (tq, LANES), 1) < SWA_HD
    pairs = SWA_GROUP // 2
    res = []
    for g, (st, _) in enumerate(scores_t):
        p_cols, inv_tiles = [], []
        for i in range(pairs):
            p_half, inv_half = [], []
            for half in range(2):
                sk = sink_ref[g * SWA_GROUP + 2 * i + half] * LOG2E
                sh = jnp.where(mask_t, st[half * nk:(half + 1) * nk, i * tq:(i + 1) * tq], -jnp.inf)
                mx = jnp.maximum(jnp.max(sh, axis=0, keepdims=True), sk)
                ph = jnp.exp2(sh - mx)
                den = jnp.sum(ph, axis=0, keepdims=True) + jnp.exp2(sk - mx)
                p_half.append(ph.astype(BF16))
                inv_row = jnp.broadcast_to(1.0 / den, (SUBLANES, tq))
                inv_half.append(jnp.transpose(inv_row)[:, 0:1])
            p_cols.append(jnp.concatenate(p_half, axis=0))
            inv_tiles.append(jnp.where(low_o, inv_half[0], inv_half[1]))
            fill()
        res.append((jnp.concatenate(p_cols, axis=1), inv_tiles))
    return res


def _swa_out_t(scores_t, probs_t):
    pairs = SWA_GROUP // 2
    tiles = []
    for (_, vbd), (pt, inv_tiles) in zip(scores_t, probs_t):
        tq = pt.shape[1] // pairs
        o2 = _dot_tn(pt, vbd)
        tiles += [o2[i * tq:(i + 1) * tq, :] * inv_tiles[i] for i in range(pairs)]
    return jnp.concatenate(tiles, axis=1).astype(BF16)


def _swa_window(kx_ref, vx_ref, r0):
    kx = kx_ref[r0:r0 + SWA_KEYS, :]
    vx = vx_ref[r0:r0 + SWA_KEYS, :]
    return kx, pltpu.roll(kx, SWA_HD, 1), vx, pltpu.roll(vx, SWA_HD, 1)


def _prompt_mask_t(history_valid):
    kj = lax.broadcasted_iota(jnp.int32, (SWA_KEYS, SWA_Q_ROWS), 0)
    qi = lax.broadcasted_iota(jnp.int32, (SWA_KEYS, SWA_Q_ROWS), 1) >> CHUNK_SHIFT
    kc = kj >> CHUNK_SHIFT
    band = (kc >= qi) & (kc <= qi + WINDOW // CHUNK)
    if history_valid is None:
        return band
    return band & ((kj >= WINDOW) | history_valid)


def _merge(x, o_gla, o_swa, gg, gs, wbg_ref, wbs_ref, wout_ref):
    merged = jax.nn.sigmoid(gg) * _dot(o_gla, wbg_ref[...]) + jax.nn.sigmoid(gs) * _dot(o_swa, wbs_ref[...])
    return x + _dot(merged.astype(BF16), wout_ref[...])


def _mix_tile(p_tile, x, kx_ref, vx_ref, row0, history_valid, refs, fill):
    sink_ref, wgu_ref, bg_ref, gn_ref, wbg_ref, wbs_ref, wout_ref, s_ref, sb_ref = refs
    p_qkg, p_gv, p_gr, p_sq, _, p_gg, p_gs = p_tile
    gq, gk, log_alpha = _gla_inputs(p_qkg[...], wgu_ref, bg_ref)
    gn = gn_ref[...]
    groups = [slice(c * SWA_Q_ROWS, (c + 1) * SWA_Q_ROWS) for c in range(MIX_TILE // SWA_Q_ROWS)]
    per_group = SWA_Q_ROWS // GLA_BLOCK
    o_gla, o_swa = [], []
    for c, qs in enumerate(groups):
        blocks = [slice(qs.start + j * GLA_BLOCK, qs.start + (j + 1) * GLA_BLOCK) for j in range(per_group)]
        scs = [_gla_scan(gq[rs], gk[rs], log_alpha[rs], fill) for rs in blocks]
        scores = _swa_scores_t(p_sq[qs, :], *_swa_window(kx_ref, vx_ref, row0 + c * SWA_Q_ROWS))
        a_heads = [_gla_scores(sc, fill) for sc in scs]
        probs = _swa_softmax_t(scores, _prompt_mask_t(history_valid if c == 0 else None), sink_ref, fill)
        for rs, sc, ah in zip(blocks, scs, a_heads):
            o_gla.append(_gla_out(sc, ah, p_gv[rs, :], s_ref, sb_ref, gn))
        o_swa.append(_swa_out_t(scores, probs))
    fill.flush()
    gr = p_gr[...]
    og = (jnp.concatenate(o_gla, axis=0) * (gr * jax.nn.sigmoid(gr))).astype(BF16)
    return _merge(x, og, jnp.concatenate(o_swa, axis=0), p_gg[...], p_gs[...], wbg_ref, wbs_ref, wout_ref)


def _mix_prompt_kernel(sink_ref, x_ref, xn_ref, g_ref, wa_ref, wr_ref, wb_ref, wgu_ref, bg_ref, gn_ref, wbg_ref, wbs_ref,
                       wout_ref, y_ref, s_ref, ck_ref, cv_ref, kx_ref, vx_ref, sb_ref, h0_ref, h1_ref, *p_refs,
                       steps_per_seq):
    step = pl.program_id(0)
    seq_step = step % steps_per_seq
    t = MIX_TILE
    n_pieces = len(_PROJ_LAYOUT)
    p_tiles = (p_refs[:n_pieces], p_refs[n_pieces:])
    h_refs = (h0_ref, h1_ref)
    win_refs = (wa_ref, wr_ref, wb_ref)
    refs = (sink_ref, wgu_ref, bg_ref, gn_ref, wbg_ref, wbs_ref, wout_ref, s_ref, sb_ref)

    @pl.when(step == 0)
    def _():
        h0_ref[...] = _rmsnorm(x_ref[0:t, :], g_ref[...]).astype(BF16)
        _Interleave(_proj_tasks(h0_ref, win_refs, p_tiles[0])).flush()

    @pl.when(seq_step == 0)
    def _():
        s_ref[...] = jnp.zeros_like(s_ref)
        sb_ref[...] = jnp.zeros_like(sb_ref)
        ck_ref[...] = jnp.zeros_like(ck_ref)
        cv_ref[...] = jnp.zeros_like(cv_ref)

    kx_ref[0:WINDOW, :] = ck_ref[...]
    vx_ref[0:WINDOW, :] = cv_ref[...]
    for i in range(2):
        p_skv = p_tiles[i][4]
        x_ahead = x_ref[t:2 * t, :] if i == 0 else xn_ref[...]
        h_refs[1 - i][...] = _rmsnorm(x_ahead, g_ref[...]).astype(BF16)
        fill = _Interleave(_proj_tasks(h_refs[1 - i], win_refs, p_tiles[1 - i]))
        kx_ref[WINDOW + i * t:WINDOW + (i + 1) * t, :] = p_skv[:, :SWA_KV_W]
        vx_ref[WINDOW + i * t:WINDOW + (i + 1) * t, :] = p_skv[:, SWA_KV_W:]
        y_ref[i * t:(i + 1) * t, :] = _mix_tile(p_tiles[i], x_ref[i * t:(i + 1) * t, :], kx_ref, vx_ref, i * t,
                                                (seq_step > 0) if i == 0 else None, refs, fill)
    ck_ref[...] = kx_ref[2 * t:2 * t + WINDOW, :]
    cv_ref[...] = vx_ref[2 * t:2 * t + WINDOW, :]


def _mix_sample_kernel(sink_ref, x_ref, g_ref, wa_ref, wr_ref, wb_ref, wgu_ref, bg_ref, gn_ref, wbg_ref, wbs_ref, wout_ref,
                       s0_ref, ck0_ref, cv0_ref, y_ref, s_ref, ck_ref, cv_ref, kx_ref, vx_ref, sb_ref, h_ref,
                       *p_refs, seq):
    x = x_ref[...]
    batch = x.shape[0] // seq
    h_ref[...] = _rmsnorm(x, g_ref[...]).astype(BF16)
    _Interleave(_proj_tasks(h_ref, (wa_ref, wr_ref, wb_ref), p_refs)).flush()
    p_qkg, p_gv, p_gr, p_sq, p_skv, p_gg, p_gs = p_refs
    no_fill = _Interleave()
    gq, gk, log_alpha = _gla_inputs(p_qkg[...], wgu_ref, bg_ref)
    gn = gn_ref[...]
    visible = lax.broadcasted_iota(jnp.int32, (seq, SWA_KEYS), 1) < WINDOW + seq
    pad = jnp.zeros((SWA_KEYS - WINDOW - seq, LANES), F32)
    o_gla, o_swa = [], []
    for b in range(batch):
        rs = slice(b * seq, (b + 1) * seq)
        s_ref[b] = s0_ref[b]
        sb_ref[b] = s0_ref[b].astype(BF16)
        sc = _gla_scan(gq[rs], gk[rs], log_alpha[rs], no_fill)
        o_gla.append(_gla_out(sc, _gla_scores(sc, no_fill), p_gv[rs, :], s_ref.at[b], sb_ref.at[b], gn))

        kx_ref[b, 0:WINDOW, :] = ck0_ref[b]
        vx_ref[b, 0:WINDOW, :] = cv0_ref[b]
        kx_ref[b, WINDOW:WINDOW + seq, :] = p_skv[rs, :SWA_KV_W]
        vx_ref[b, WINDOW:WINDOW + seq, :] = p_skv[rs, SWA_KV_W:]
        kx_ref[b, WINDOW + seq:, :] = pad
        vx_ref[b, WINDOW + seq:, :] = pad
        ck_ref[b] = kx_ref[b, seq:seq + WINDOW, :]
        cv_ref[b] = vx_ref[b, seq:seq + WINDOW, :]
        scores = _swa_scores(p_sq[rs, :], *_swa_window(kx_ref.at[b], vx_ref.at[b], 0))
        o_swa.append(_swa_out(scores, _swa_softmax(scores, visible, sink_ref, no_fill)))
    gr = p_gr[...]
    og = (jnp.concatenate(o_gla, axis=0) * (gr * jax.nn.sigmoid(gr))).astype(BF16)
    y_ref[...] = _merge(x, og, jnp.concatenate(o_swa, axis=0), p_gg[...], p_gs[...], wbg_ref, wbs_ref, wout_ref)


def _mix_weight_specs(win, wgu, wbg, wbs, wout):
    return [_resident((1, D_MODEL))] + [_resident(w.shape) for w in win] + [
        _resident(wgu.shape),
        _resident((1, GLA_QK_W)),
        _resident((1, GLA_DV)),
        _resident(wbg.shape),
        _resident(wbs.shape),
        _resident(wout.shape),
    ]


def _mix_out_shapes(batch, seq):
    return [
        jax.ShapeDtypeStruct((batch * seq, D_MODEL), F32),
        jax.ShapeDtypeStruct((batch, GLA_HEADS, GLA_DK, GLA_DV), F32),
        jax.ShapeDtypeStruct((batch, WINDOW, SWA_KV_W), F32),
        jax.ShapeDtypeStruct((batch, WINDOW, SWA_KV_W), F32),
    ]


def _mix_scratch(key_rows):
    return [pltpu.VMEM((key_rows, LANES), F32), pltpu.VMEM((key_rows, LANES), F32),
            pltpu.VMEM((GLA_HEADS, GLA_DK, GLA_DV), BF16)]


def _mix_prompt(x, batch, seq, sinks, g, win, wgu, bg, gn, wbg, wbs, wout):
    rows = 2 * MIX_TILE
    steps_per_seq = seq // rows
    n_steps = batch * steps_per_seq
    last_tile = batch * seq // MIX_TILE - 1
    state_spec = pl.BlockSpec((None, GLA_HEADS, GLA_DK, GLA_DV), lambda i: (i // steps_per_seq, 0, 0, 0))
    cache_spec = pl.BlockSpec((None, WINDOW, SWA_KV_W), lambda i: (i // steps_per_seq, 0, 0))
    return pl.pallas_call(
        functools.partial(_mix_prompt_kernel, steps_per_seq=steps_per_seq),
        grid=(n_steps,),
        in_specs=[
            pl.BlockSpec(memory_space=pltpu.SMEM),
            pl.BlockSpec((rows, D_MODEL), lambda i: (i, 0)),
            pl.BlockSpec((MIX_TILE, D_MODEL), lambda i: (jnp.minimum(2 * i + 2, last_tile), 0)),
        ] + _mix_weight_specs(win, wgu, wbg, wbs, wout),
        out_specs=[pl.BlockSpec((rows, D_MODEL), lambda i: (i, 0)), state_spec, cache_spec, cache_spec],
        out_shape=_mix_out_shapes(batch, seq),
        scratch_shapes=_mix_scratch(WINDOW + rows)
        + [pltpu.VMEM((MIX_TILE, D_MODEL), BF16)] * 2
        + [pltpu.VMEM((MIX_TILE, w), dt) for w, dt in _PROJ_SHAPES] * 2,
        compiler_params=pltpu.CompilerParams(
            dimension_semantics=("arbitrary",), vmem_limit_bytes=VMEM_LIMIT),
    )(sinks, x, x, g, *win, wgu, bg, gn, wbg, wbs, wout)


def _mix_sample(x, batch, seq, sinks, g, win, wgu, bg, gn, wbg, wbs, wout, state0, ck0, cv0):
    rows = batch * seq
    whole = lambda a: pl.BlockSpec(a.shape, lambda i: (0,) * a.ndim)
    out_shape = _mix_out_shapes(batch, seq)
    return pl.pallas_call(
        functools.partial(_mix_sample_kernel, seq=seq),
        grid=(1,),
        in_specs=[pl.BlockSpec(memory_space=pltpu.SMEM), whole(x)]
        + _mix_weight_specs(win, wgu, wbg, wbs, wout) + [whole(state0), whole(ck0), whole(cv0)],
        out_specs=[whole(o) for o in out_shape],
        out_shape=out_shape,
        scratch_shapes=[pltpu.VMEM((batch, SWA_KEYS, LANES), F32), pltpu.VMEM((batch, SWA_KEYS, LANES), F32),
                        pltpu.VMEM((batch, GLA_HEADS, GLA_DK, GLA_DV), BF16), pltpu.VMEM((rows, D_MODEL), BF16)]
        + [pltpu.VMEM((rows, w), dt) for w, dt in _PROJ_SHAPES],
        compiler_params=pltpu.CompilerParams(
            dimension_semantics=("arbitrary",), vmem_limit_bytes=VMEM_LIMIT),
    )(sinks, x, g, *win, wgu, bg, gn, wbg, wbs, wout, state0, ck0, cv0)


def _ffn_weights(norm_g, w_in, w_out):
    return norm_g.reshape(1, D_MODEL), w_in.astype(BF16), w_out.astype(BF16)


def _mix_weights(w_in, w_gate_up):
    rank0 = IN_HEAD_W
    head = w_in[:, :rank0].astype(BF16)
    rank = jnp.pad(w_in[:, rank0:rank0 + GLA_RANK], ((0, 0), (0, RANK_PAD - GLA_RANK))).astype(BF16)
    tail = w_in[:, rank0 + GLA_RANK:].astype(BF16)
    assert tail.shape[1] == IN_TAIL_W
    wgu = jnp.pad(w_gate_up, ((0, RANK_PAD - GLA_RANK), (0, 0))).astype(BF16)
    return (head, rank, tail), wgu


def _layers(xp, xs, ffn1, mixw, ffn2, gf, state0, ck0, cv0):
    (bp, sp, _), (bs, ss, _) = xp.shape, xs.shape
    x1p, x1s = _ffn(xp.reshape(bp * sp, D_MODEL), xs.reshape(bs * ss, D_MODEL), *ffn1, gf, final_norm=False)
    x2p, *carry_p = _mix_prompt(x1p, bp, sp, *mixw)
    x2s, *carry_s = _mix_sample(x1s, bs, ss, *mixw, state0, ck0, cv0)
    yp, ys = _ffn(x2p, x2s, *ffn2, gf, final_norm=True)

    def carry(batch, state, ck, cv):
        return (state[None], ck.reshape(1, batch, WINDOW, SWA_KV_HEADS, SWA_HD),
                cv.reshape(1, batch, WINDOW, SWA_KV_HEADS, SWA_HD))

    return (yp.reshape(bp, sp, D_MODEL), ys.reshape(bs, ss, D_MODEL), *carry(bp, *carry_p), *carry(bs, *carry_s))


def kernel(x_prompt, x_sample, state_gla, cache_swa_k, cache_swa_v, norm_ffn1, w_ffn1_in, w_ffn1_out, norm_mix, w_in, w_gla_gate_up, b_gla_gate, gla_norm, swa_sinks, w_branch_gla, w_branch_swa, w_out, norm_ffn2, w_ffn2_in, w_ffn2_out, norm_final):
    ffn1 = _ffn_weights(norm_ffn1[0], w_ffn1_in[0], w_ffn1_out[0])
    ffn2 = _ffn_weights(norm_ffn2[0], w_ffn2_in[0], w_ffn2_out[0])
    win, wgu = _mix_weights(w_in[0], w_gla_gate_up[0])
    mixw = (swa_sinks[0], norm_mix[0].reshape(1, D_MODEL), win, wgu, b_gla_gate[0].reshape(1, GLA_QK_W),
            gla_norm[0].reshape(1, GLA_DV), w_branch_gla[0].astype(BF16), w_branch_swa[0].astype(BF16),
            w_out[0].astype(BF16))
    gf = norm_final.reshape(1, D_MODEL)
    dec_batch = x_sample.shape[0]
    return _layers(x_prompt, x_sample, ffn1, mixw, ffn2, gf, state_gla[0],
                   cache_swa_k[0].reshape(dec_batch, WINDOW, SWA_KV_W),
                   cache_swa_v[0].reshape(dec_batch, WINDOW, SWA_KV_W))
```

```python
import functools

import jax
import jax.numpy as jnp
from jax import lax
from jax.experimental import pallas as pl
from jax.experimental.pallas import tpu as pltpu

F32 = jnp.float32
BF16 = jnp.bfloat16

D_MODEL = 1024
D_FF = 2816
CHUNK = 64
CHUNK_SHIFT = CHUNK.bit_length() - 1
EPS = 1e-6
LOG2E = 1.4426950408889634
GLA_HEADS = 4
GLA_DK = 128
GLA_DV = 256
GLA_RANK = 16
GLA_TAU = 16.0
SWA_HEADS = 16
SWA_KV_HEADS = 2
SWA_HD = 64
SWA_GROUP = SWA_HEADS // SWA_KV_HEADS
WINDOW = 128
GLA_QK_W = GLA_HEADS * GLA_DK
GLA_V_W = GLA_HEADS * GLA_DV
SWA_Q_W = SWA_HEADS * SWA_HD
SWA_KV_W = SWA_KV_HEADS * SWA_HD

LANES = 128
RANK_PAD = LANES
FF_CHUNK = 256
N_FF_CHUNKS = D_FF // FF_CHUNK
VMEM_LIMIT = 56 * 1024 * 1024

FFN_ROWS = 512
MIX_TILE = 256
GLA_BLOCK = 64
SWA_Q_ROWS = 2 * CHUNK
SWA_KEYS = 2 * WINDOW

IN_HEAD_W = 2 * GLA_QK_W + 2 * GLA_V_W
IN_TAIL_W = SWA_Q_W + 2 * SWA_KV_W + 2 * D_MODEL
_W_HEAD, _W_RANK, _W_TAIL = range(3)
_C_GQ = 0
_C_GK = _C_GQ + GLA_QK_W
_C_GLR = _C_GK + GLA_QK_W

_PROJ_LAYOUT = (
    (F32, None, ((_W_HEAD, 0, 2 * GLA_QK_W), (_W_RANK, 0, RANK_PAD))),
    (BF16, None, ((_W_HEAD, 2 * GLA_QK_W, GLA_V_W),)),
    (F32, None, ((_W_HEAD, 2 * GLA_QK_W + GLA_V_W, GLA_V_W),)),
    (BF16, LOG2E * SWA_HD ** -0.5, ((_W_TAIL, 0, SWA_Q_W),)),
    (F32, None, ((_W_TAIL, SWA_Q_W, 2 * SWA_KV_W),)),
    (F32, None, ((_W_TAIL, SWA_Q_W + 2 * SWA_KV_W, D_MODEL),)),
    (F32, None, ((_W_TAIL, SWA_Q_W + 2 * SWA_KV_W + D_MODEL, D_MODEL),)),
)
_PROJ_SHAPES = tuple((sum(w for _, _, w in segs), dt) for dt, _, segs in _PROJ_LAYOUT)
PROJ_TASK_COLS = 256


def _rmsnorm(x, g):
    ms = jnp.mean(x * x, axis=-1, keepdims=True)
    return x * lax.rsqrt(ms + EPS) * g


def _dot(a, b):
    return jnp.dot(a, b, preferred_element_type=F32)


def _dot_nt(a, b):
    return lax.dot_general(a, b, (((1,), (1,)), ((), ())), preferred_element_type=F32)


def _dot_tn(a, b):
    return lax.dot_general(a, b, (((0,), (0,)), ((), ())), preferred_element_type=F32)


def _ffn_rows(x_ref, o_ref, g_ref, w1_ref, w2_ref, gf_ref, h_ref, acc_ref, final_norm, ahead=None):
    x = x_ref[...]
    n = x.shape[0]
    if ahead is None:
        h_ref[...] = _rmsnorm(x, g_ref[...]).astype(BF16)
    else:
        xa_ref, ha_ref = ahead
        per_chunk = -(-n // (N_FF_CHUNKS * 16)) * 16
    for c in range(N_FF_CHUNKS):
        cols = slice(c * FF_CHUNK, (c + 1) * FF_CHUNK)
        up_cols = slice(D_FF + c * FF_CHUNK, D_FF + (c + 1) * FF_CHUNK)
        h = h_ref[...]
        gate = _dot(h, w1_ref[:, cols])
        up = _dot(h, w1_ref[:, up_cols])
        act = (gate * jax.nn.sigmoid(gate) * up).astype(BF16)
        down = _dot(act, w2_ref[cols, :])
        if c == 0:
            acc_ref[...] = down
        else:
            acc_ref[...] += down
        if ahead is not None:
            rs = slice(min(n, c * per_chunk), min(n, (c + 1) * per_chunk))
            if rs.start < rs.stop:
                ha_ref[rs, :] = _rmsnorm(xa_ref[rs, :], g_ref[...]).astype(BF16)
    y = x + 0.5 * acc_ref[...]
    if final_norm:
        y = _rmsnorm(y, gf_ref[...])
    o_ref[...] = y


def _cast_blocks(jobs, in_refs, out_refs):
    outs = iter(out_refs)
    for ranges, in_ref in zip(jobs, in_refs):
        for c0, c1, width in ranges:
            o_ref = next(outs)
            if c1 - c0 == width:
                o_ref[...] = in_ref[:, c0:c1].astype(BF16)
            else:
                tile = in_ref[:, c0:c0 + width]
                keep = lax.broadcasted_iota(jnp.int32, tile.shape, 1) < c1 - c0
                o_ref[...] = jnp.where(keep, tile, 0.0).astype(BF16)


def _ffn_kernel(*refs, final_norm, prompt_steps, cast_jobs):
    n_in = len(cast_jobs)
    n_out = sum(len(r) for r in cast_jobs)
    xp_ref, xa_ref, xs_ref, g_ref, w1_ref, w2_ref, gf_ref = refs[:7]
    cast_in = refs[7:7 + n_in]
    yp_ref, ys_ref = refs[7 + n_in:9 + n_in]
    cast_out = refs[9 + n_in:9 + n_in + n_out]
    h_ref, acc_ref = refs[9 + n_in + n_out:]
    step = pl.program_id(0)
    rows_s = xs_ref.shape[0]
    slot = step % 2
    _cast_blocks(cast_jobs, cast_in, cast_out)

    @pl.when(step == 0)
    def _():
        h_ref[0] = _rmsnorm(xp_ref[...], g_ref[...]).astype(BF16)

    @pl.when(step < prompt_steps)
    def _():
        _ffn_rows(xp_ref, yp_ref, g_ref, w1_ref, w2_ref, gf_ref, h_ref.at[slot], acc_ref, final_norm,
                  ahead=(xa_ref, h_ref.at[1 - slot]))

    @pl.when(step == prompt_steps)
    def _():
        _ffn_rows(xs_ref, ys_ref, g_ref, w1_ref, w2_ref, gf_ref,
                  h_ref.at[0, 0:rows_s], acc_ref.at[0:rows_s], final_norm)


def _resident(shape):
    nd = len(shape)
    return pl.BlockSpec(shape, lambda *_: (0,) * nd, pipeline_mode=pl.Buffered(1))


def _ffn(xp, xs, g, w1, w2, gf, *, final_norm, casts=()):
    n_p, n_s = xp.shape[0], xs.shape[0]
    assert n_p % FFN_ROWS == 0 and n_s <= FFN_ROWS
    prompt_steps = n_p // FFN_ROWS
    prompt_spec = pl.BlockSpec((FFN_ROWS, D_MODEL), lambda i: (jnp.minimum(i, prompt_steps - 1), 0))
    sample_spec = pl.BlockSpec((n_s, D_MODEL), lambda i: (0, 0))
    cast_in_specs, cast_out_specs, cast_out_shapes = [], [], []
    for a, rb, ranges in casts:
        n_blocks = a.shape[0] // rb
        assert a.shape[0] % rb == 0 and n_blocks <= prompt_steps + 1
        index = functools.partial(lambda i, last: (jnp.minimum(i, last), 0), last=n_blocks - 1)
        cast_in_specs.append(pl.BlockSpec((rb, a.shape[1]), index))
        for _, _, width in ranges:
            cast_out_specs.append(pl.BlockSpec((rb, width), index))
            cast_out_shapes.append(jax.ShapeDtypeStruct((a.shape[0], width), BF16))
    return pl.pallas_call(
        functools.partial(_ffn_kernel, final_norm=final_norm, prompt_steps=prompt_steps,
                          cast_jobs=tuple(r for _, _, r in casts)),
        grid=(prompt_steps + 1,),
        in_specs=[
            prompt_spec,
            pl.BlockSpec((FFN_ROWS, D_MODEL), lambda i: (jnp.minimum(i + 1, prompt_steps - 1), 0)),
            sample_spec,
            _resident((1, D_MODEL)),
            _resident(w1.shape),
            _resident(w2.shape),
            _resident((1, D_MODEL)),
        ] + cast_in_specs,
        out_specs=[prompt_spec, sample_spec] + cast_out_specs,
        out_shape=[jax.ShapeDtypeStruct((n_p, D_MODEL), F32), jax.ShapeDtypeStruct((n_s, D_MODEL), F32)]
        + cast_out_shapes,
        scratch_shapes=[pltpu.VMEM((2, FFN_ROWS, D_MODEL), BF16), pltpu.VMEM((FFN_ROWS, D_MODEL), F32)],
        compiler_params=pltpu.CompilerParams(
            dimension_semantics=("arbitrary",), vmem_limit_bytes=VMEM_LIMIT),
    )(xp, xp, xs, g, w1, w2, gf, *[a for a, _, _ in casts])


def _proj_tasks(h_ref, win_refs, p_refs):
    tasks = []
    for dst, (dtype, scale, segments) in zip(p_refs, _PROJ_LAYOUT):
        d0 = 0
        for part, col0, width in segments:
            for c in range(0, width, PROJ_TASK_COLS):
                w = min(PROJ_TASK_COLS, width - c)

                def task(dst=dst, d=d0 + c, w=w, src=win_refs[part], s=col0 + c, dtype=dtype, scale=scale):
                    v = _dot(h_ref[...], src[:, s:s + w])
                    if scale is not None:
                        v = v * scale
                    dst[:, d:d + w] = v.astype(dtype)

                tasks.append(task)
            d0 += width
    return tasks


class _Interleave:
    def __init__(self, tasks=()):
        self._tasks = list(tasks)

    def __call__(self, n=1):
        for _ in range(n):
            if self._tasks:
                self._tasks.pop(0)()

    def flush(self):
        self(len(self._tasks))


SUBLANES = 8


def _ref_rows(p, m, row):
    n_rows, width = p.shape
    if 4 * m <= SUBLANES:
        groups = p.reshape(n_rows // SUBLANES, SUBLANES, width)
        back = lambda k: pltpu.roll(groups, k % SUBLANES, 1).reshape(n_rows, width)
        if m == 1:
            return jnp.where((row & 1) == 0, p, back(1))
        r = row & 3
        return jnp.where(r == 0, back(-1), jnp.where(r == 1, p, jnp.where(r == 2, back(1), back(2))))
    nb = n_rows // (2 * m)
    p3 = p.reshape(nb, 2 * m, width)
    r3 = jnp.broadcast_to(p3[:, m - 1:m, :], (nb, 2 * m, width))
    return r3.reshape(n_rows, width)


def _gla_scan(q, k, la, fill):
    n_rows = q.shape[0]
    row = lax.broadcasted_iota(jnp.int32, (n_rows, 1), 0)
    p = la
    zs = []
    m = 1
    while m < n_rows:
        upper = (row & m) != 0
        r = _ref_rows(p, m, row)
        e = jnp.exp2(jnp.where(upper, p, r - p))
        zs.append((jnp.where(upper, q, k) * e).astype(BF16))
        p = p + jnp.where(upper, r, 0.0)
        fill()
        m *= 2
    b = p
    b_last = b[n_rows - 1:n_rows, :]
    return dict(zs=zs, qb=q.astype(BF16), kb=k.astype(BF16),
                q_in=(q * jnp.exp2(b)).astype(BF16), k_out=(k * jnp.exp2(b_last - b)).astype(BF16),
                s_decay=jnp.exp2(b_last))


def _gla_scores(sc, fill):
    n_rows = sc["qb"].shape[0]
    n_cols = max(n_rows, LANES)
    ti = lax.broadcasted_iota(jnp.int32, (n_rows, n_cols), 0)
    si = lax.broadcasted_iota(jnp.int32, (n_rows, n_cols), 1)
    split_level = 31 - lax.clz(jnp.where(ti > si, ti ^ si, 0))
    eye = ti == si
    pad = jnp.zeros((n_cols - n_rows, GLA_DK), BF16)

    def keys(x):
        return x if n_cols == n_rows else jnp.concatenate([x, pad], axis=0)

    a_heads = []
    for h in range(GLA_HEADS):
        ks = slice(h * GLA_DK, (h + 1) * GLA_DK)
        a = jnp.where(eye, _dot_nt(sc["qb"][:, ks], keys(sc["kb"][:, ks])), 0.0)
        for li, z in enumerate(sc["zs"]):
            a = jnp.where(split_level == li, _dot_nt(z[:, ks], keys(z[:, ks])), a)
        a_heads.append(a.astype(BF16))
        fill()
    return a_heads


def _gla_out(sc, a_heads, vb, s_ref, sb_ref, gnorm):
    n_rows = vb.shape[0]
    pad = jnp.zeros((a_heads[0].shape[1] - n_rows, GLA_DV), BF16)
    outs = []
    for h in range(GLA_HEADS):
        ks = slice(h * GLA_DK, (h + 1) * GLA_DK)
        vs = slice(h * GLA_DV, (h + 1) * GLA_DV)
        rhs = [vb[:, vs], sb_ref[h]] if pad.shape[0] == 0 else [vb[:, vs], pad, sb_ref[h]]
        o = _dot(jnp.concatenate([a_heads[h], sc["q_in"][:, ks]], axis=1), jnp.concatenate(rhs, axis=0))
        dcol = jnp.transpose(jnp.broadcast_to(sc["s_decay"][:, ks], (SUBLANES, GLA_DK)))[:, 0:1]
        s_new = dcol * s_ref[h] + _dot_tn(sc["k_out"][:, ks], vb[:, vs])
        s_ref[h] = s_new
        sb_ref[h] = s_new.astype(BF16)
        outs.append(_rmsnorm(o, gnorm))
    return jnp.concatenate(outs, axis=1)


def _gla_inputs(qkg, wgu_ref, bg_ref):
    logit = _dot(qkg[:, _C_GLR:_C_GLR + RANK_PAD].astype(BF16), wgu_ref[...]) + bg_ref[...]
    log_alpha = (jnp.minimum(logit, 0.0) - jnp.log1p(jnp.exp(-jnp.abs(logit)))) * (LOG2E / GLA_TAU)
    gq = qkg[:, _C_GQ:_C_GQ + GLA_QK_W] * (GLA_DK ** -0.5)
    gk = qkg[:, _C_GK:_C_GK + GLA_QK_W]
    return gq, gk, log_alpha


def _swa_operands(qb, kn, ks, vn, vs):
    nk = kn.shape[0]
    low = lax.broadcasted_iota(jnp.int32, (nk, LANES), 1) < SWA_HD
    pairs = SWA_GROUP // 2
    res = []
    for g in range(SWA_KV_HEADS):
        k_lo, k_hi = (kn, ks) if g == 0 else (ks, kn)
        v_lo, v_hi = (vn, vs) if g == 0 else (vs, vn)
        kbd = jnp.concatenate([jnp.where(low, k_lo, 0.0), jnp.where(low, 0.0, k_hi)], axis=0).astype(BF16)
        vbd = jnp.concatenate([jnp.where(low, v_lo, 0.0), jnp.where(low, 0.0, v_hi)], axis=0).astype(BF16)
        qs = jnp.concatenate(
            [qb[:, (g * pairs + i) * LANES:(g * pairs + i + 1) * LANES] for i in range(pairs)], axis=0)
        res.append((kbd, qs, vbd))
    return res


def _swa_scores(qb, kn, ks, vn, vs):
    return [(_dot_nt(qs, kbd), vbd) for kbd, qs, vbd in _swa_operands(qb, kn, ks, vn, vs)]


def _swa_softmax(scores, mask, sink_ref, fill):
    tq, nk = mask.shape
    low_o = lax.broadcasted_iota(jnp.int32, (tq, LANES), 1) < SWA_HD
    pairs = SWA_GROUP // 2
    res = []
    for g, (s, _) in enumerate(scores):
        p_rows, inv_rows = [], []
        for i in range(pairs):
            p_half, inv_half = [], []
            for half in range(2):
                sk = sink_ref[g * SWA_GROUP + 2 * i + half] * LOG2E
                sh = jnp.where(mask, s[i * tq:(i + 1) * tq, half * nk:(half + 1) * nk], -jnp.inf)
                mx = jnp.maximum(jnp.max(sh, axis=-1, keepdims=True), sk)
                ph = jnp.exp2(sh - mx)
                den = jnp.sum(ph, axis=-1, keepdims=True) + jnp.exp2(sk - mx)
                p_half.append(ph.astype(BF16))
                inv_half.append(1.0 / den)
            p_rows.append(jnp.concatenate(p_half, axis=1))
            inv_rows.append(jnp.where(low_o, inv_half[0], inv_half[1]))
            fill()
        res.append((jnp.concatenate(p_rows, axis=0), inv_rows))
    return res


def _swa_out(scores, probs):
    pairs = SWA_GROUP // 2
    tiles = []
    for (_, vbd), (p, inv_rows) in zip(scores, probs):
        tq = p.shape[0] // pairs
        o2 = _dot(p, vbd)
        tiles += [o2[i * tq:(i + 1) * tq, :] * inv_rows[i] for i in range(pairs)]
    return jnp.concatenate(tiles, axis=1).astype(BF16)


def _swa_scores_t(qb, kn, ks, vn, vs):
    return [(_dot_nt(kbd, qs), vbd) for kbd, qs, vbd in _swa_operands(qb, kn, ks, vn, vs)]


def _swa_softmax_t(scores_t, mask_t, sink_ref, fill):
    nk, tq = mask_t.shape
    low_o = lax.broadcasted_iota(jnp.int32, (tq, LANES), 1) < SWA_HD
    pairs = SWA_GROUP // 2
    res = []
    for g, (st, _) in enumerate(scores_t):
        p_cols, inv_tiles = [], []
        for i in range(pairs):
            p_half, inv_half = [], []
            for half in range(2):
                sk = sink_ref[g * SWA_GROUP + 2 * i + half] * LOG2E
                sh = jnp.where(mask_t, st[half * nk:(half + 1) * nk, i * tq:(i + 1) * tq], -jnp.inf)
                mx = jnp.maximum(jnp.max(sh, axis=0, keepdims=True), sk)
                ph = jnp.exp2(sh - mx)
                den = jnp.sum(ph, axis=0, keepdims=True) + jnp.exp2(sk - mx)
                p_half.append(ph.astype(BF16))
                inv_row = jnp.broadcast_to(1.0 / den, (SUBLANES, tq))
                inv_half.append(jnp.transpose(inv_row)[:, 0:1])
            p_cols.append(jnp.concatenate(p_half, axis=0))
            inv_tiles.append(jnp.where(low_o, inv_half[0], inv_half[1]))
            fill()
        res.append((jnp.concatenate(p_cols, axis=1), inv_tiles))
    return res


def _swa_out_t(scores_t, probs_t):
    pairs = SWA_GROUP // 2
    tiles = []
    for (_, vbd), (pt, inv_tiles) in zip(scores_t, probs_t):
        tq = pt.shape[1] // pairs
        o2 = _dot_tn(pt, vbd)
        tiles += [o2[i * tq:(i + 1) * tq, :] * inv_tiles[i] for i in range(pairs)]
    return jnp.concatenate(tiles, axis=1).astype(BF16)


def _swa_window(kx_ref, vx_ref, r0):
    kx = kx_ref[r0:r0 + SWA_KEYS, :]
    vx = vx_ref[r0:r0 + SWA_KEYS, :]
    return kx, pltpu.roll(kx, SWA_HD, 1), vx, pltpu.roll(vx, SWA_HD, 1)


def _prompt_mask_t(history_valid):
    kj = lax.broadcasted_iota(jnp.int32, (SWA_KEYS, SWA_Q_ROWS), 0)
    qi = lax.broadcasted_iota(jnp.int32, (SWA_KEYS, SWA_Q_ROWS), 1) >> CHUNK_SHIFT
    kc = kj >> CHUNK_SHIFT
    band = (kc >= qi) & (kc <= qi + WINDOW // CHUNK)
    if history_valid is None:
        return band
    return band & ((kj >= WINDOW) | history_valid)


def _merge(x, o_gla, o_swa, gg, gs, wbg_ref, wbs_ref, wout_ref):
    merged = jax.nn.sigmoid(gg) * _dot(o_gla, wbg_ref[...]) + jax.nn.sigmoid(gs) * _dot(o_swa, wbs_ref[...])
    return x + _dot(merged.astype(BF16), wout_ref[...])


def _mix_tile(p_tile, x, kx_ref, vx_ref, row0, history_valid, refs, fill):
    sink_ref, wgu_ref, bg_ref, gn_ref, wbg_ref, wbs_ref, wout_ref, s_ref, sb_ref = refs
    p_qkg, p_gv, p_gr, p_sq, _, p_gg, p_gs = p_tile
    gq, gk, log_alpha = _gla_inputs(p_qkg[...], wgu_ref, bg_ref)
    gn = gn_ref[...]
    groups = [slice(c * SWA_Q_ROWS, (c + 1) * SWA_Q_ROWS) for c in range(MIX_TILE // SWA_Q_ROWS)]
    per_group = SWA_Q_ROWS // GLA_BLOCK
    o_gla, o_swa = [], []
    for c, qs in enumerate(groups):
        blocks = [slice(qs.start + j * GLA_BLOCK, qs.start + (j + 1) * GLA_BLOCK) for j in range(per_group)]
        scs = [_gla_scan(gq[rs], gk[rs], log_alpha[rs], fill) for rs in blocks]
        scores = _swa_scores_t(p_sq[qs, :], *_swa_window(kx_ref, vx_ref, row0 + c * SWA_Q_ROWS))
        a_heads = [_gla_scores(sc, fill) for sc in scs]
        probs = _swa_softmax_t(scores, _prompt_mask_t(history_valid if c == 0 else None), sink_ref, fill)
        for rs, sc, ah in zip(blocks, scs, a_heads):
            o_gla.append(_gla_out(sc, ah, p_gv[rs, :], s_ref, sb_ref, gn))
        o_swa.append(_swa_out_t(scores, probs))
    fill.flush()
    gr = p_gr[...]
    og = (jnp.concatenate(o_gla, axis=0) * (gr * jax.nn.sigmoid(gr))).astype(BF16)
    return _merge(x, og, jnp.concatenate(o_swa, axis=0), p_gg[...], p_gs[...], wbg_ref, wbs_ref, wout_ref)


def _mix_prompt_kernel(sink_ref, x_ref, xn_ref, g_ref, wa_ref, wr_ref, wb_ref, wgu_ref, bg_ref, gn_ref, wbg_ref, wbs_ref,
                       wout_ref, y_ref, s_ref, ck_ref, cv_ref, kx_ref, vx_ref, sb_ref, h0_ref, h1_ref, *p_refs,
                       steps_per_seq):
    step = pl.program_id(0)
    seq_step = step % steps_per_seq
    t = MIX_TILE
    n_pieces = len(_PROJ_LAYOUT)
    p_tiles = (p_refs[:n_pieces], p_refs[n_pieces:])
    h_refs = (h0_ref, h1_ref)
    win_refs = (wa_ref, wr_ref, wb_ref)
    refs = (sink_ref, wgu_ref, bg_ref, gn_ref, wbg_ref, wbs_ref, wout_ref, s_ref, sb_ref)

    @pl.when(step == 0)
    def _():
        h0_ref[...] = _rmsnorm(x_ref[0:t, :], g_ref[...]).astype(BF16)
        _Interleave(_proj_tasks(h0_ref, win_refs, p_tiles[0])).flush()

    @pl.when(seq_step == 0)
    def _():
        s_ref[...] = jnp.zeros_like(s_ref)
        sb_ref[...] = jnp.zeros_like(sb_ref)
        ck_ref[...] = jnp.zeros_like(ck_ref)
        cv_ref[...] = jnp.zeros_like(cv_ref)

    kx_ref[0:WINDOW, :] = ck_ref[...]
    vx_ref[0:WINDOW, :] = cv_ref[...]
    for i in range(2):
        p_skv = p_tiles[i][4]
        x_ahead = x_ref[t:2 * t, :] if i == 0 else xn_ref[...]
        h_refs[1 - i][...] = _rmsnorm(x_ahead, g_ref[...]).astype(BF16)
        fill = _Interleave(_proj_tasks(h_refs[1 - i], win_refs, p_tiles[1 - i]))
        kx_ref[WINDOW + i * t:WINDOW + (i + 1) * t, :] = p_skv[:, :SWA_KV_W]
        vx_ref[WINDOW + i * t:WINDOW + (i + 1) * t, :] = p_skv[:, SWA_KV_W:]
        y_ref[i * t:(i + 1) * t, :] = _mix_tile(p_tiles[i], x_ref[i * t:(i + 1) * t, :], kx_ref, vx_ref, i * t,
                                                (seq_step > 0) if i == 0 else None, refs, fill)
    ck_ref[...] = kx_ref[2 * t:2 * t + WINDOW, :]
    cv_ref[...] = vx_ref[2 * t:2 * t + WINDOW, :]


def _mix_sample_kernel(sink_ref, x_ref, g_ref, wa_ref, wr_ref, wb_ref, wgu_ref, bg_ref, gn_ref, wbg_ref, wbs_ref, wout_ref,
                       s0_ref, ck0_ref, cv0_ref, y_ref, s_ref, ck_ref, cv_ref, kx_ref, vx_ref, sb_ref, h_ref,
                       *p_refs, seq):
    x = x_ref[...]
    batch = x.shape[0] // seq
    h_ref[...] = _rmsnorm(x, g_ref[...]).astype(BF16)
    _Interleave(_proj_tasks(h_ref, (wa_ref, wr_ref, wb_ref), p_refs)).flush()
    p_qkg, p_gv, p_gr, p_sq, p_skv, p_gg, p_gs = p_refs
    no_fill = _Interleave()
    gq, gk, log_alpha = _gla_inputs(p_qkg[...], wgu_ref, bg_ref)
    gn = gn_ref[...]
    visible = lax.broadcasted_iota(jnp.int32, (seq, SWA_KEYS), 1) < WINDOW + seq
    pad = jnp.zeros((SWA_KEYS - WINDOW - seq, LANES), F32)
    o_gla, o_swa = [], []
    for b in range(batch):
        rs = slice(b * seq, (b + 1) * seq)
        s_ref[b] = s0_ref[b]
        sb_ref[b] = s0_ref[b].astype(BF16)
        sc = _gla_scan(gq[rs], gk[rs], log_alpha[rs], no_fill)
        o_gla.append(_gla_out(sc, _gla_scores(sc, no_fill), p_gv[rs, :], s_ref.at[b], sb_ref.at[b], gn))

        kx_ref[b, 0:WINDOW, :] = ck0_ref[b]
        vx_ref[b, 0:WINDOW, :] = cv0_ref[b]
        kx_ref[b, WINDOW:WINDOW + seq, :] = p_skv[rs, :SWA_KV_W]
        vx_ref[b, WINDOW:WINDOW + seq, :] = p_skv[rs, SWA_KV_W:]
        kx_ref[b, WINDOW + seq:, :] = pad
        vx_ref[b, WINDOW + seq:, :] = pad
        ck_ref[b] = kx_ref[b, seq:seq + WINDOW, :]
        cv_ref[b] = vx_ref[b, seq:seq + WINDOW, :]
        scores = _swa_scores(p_sq[rs, :], *_swa_window(kx_ref.at[b], vx_ref.at[b], 0))
        o_swa.append(_swa_out(scores, _swa_softmax(scores, visible, sink_ref, no_fill)))
    gr = p_gr[...]
    og = (jnp.concatenate(o_gla, axis=0) * (gr * jax.nn.sigmoid(gr))).astype(BF16)
    y_ref[...] = _merge(x, og, jnp.concatenate(o_swa, axis=0), p_gg[...], p_gs[...], wbg_ref, wbs_ref, wout_ref)


def _mix_weight_specs(win, wgu, wbg, wbs, wout):
    return [_resident((1, D_MODEL))] + [_resident(w.shape) for w in win] + [
        _resident(wgu.shape),
        _resident((1, GLA_QK_W)),
        _resident((1, GLA_DV)),
        _resident(wbg.shape),
        _resident(wbs.shape),
        _resident(wout.shape),
    ]


def _mix_out_shapes(batch, seq):
    return [
        jax.ShapeDtypeStruct((batch * seq, D_MODEL), F32),
        jax.ShapeDtypeStruct((batch, GLA_HEADS, GLA_DK, GLA_DV), F32),
        jax.ShapeDtypeStruct((batch, WINDOW, SWA_KV_W), F32),
        jax.ShapeDtypeStruct((batch, WINDOW, SWA_KV_W), F32),
    ]


def _mix_scratch(key_rows):
    return [pltpu.VMEM((key_rows, LANES), F32), pltpu.VMEM((key_rows, LANES), F32),
            pltpu.VMEM((GLA_HEADS, GLA_DK, GLA_DV), BF16)]


def _mix_prompt(x, batch, seq, sinks, g, win, wgu, bg, gn, wbg, wbs, wout):
    rows = 2 * MIX_TILE
    steps_per_seq = seq // rows
    n_steps = batch * steps_per_seq
    last_tile = batch * seq // MIX_TILE - 1
    state_spec = pl.BlockSpec((None, GLA_HEADS, GLA_DK, GLA_DV), lambda i: (i // steps_per_seq, 0, 0, 0))
    cache_spec = pl.BlockSpec((None, WINDOW, SWA_KV_W), lambda i: (i // steps_per_seq, 0, 0))
    return pl.pallas_call(
        functools.partial(_mix_prompt_kernel, steps_per_seq=steps_per_seq),
        grid=(n_steps,),
        in_specs=[
            pl.BlockSpec(memory_space=pltpu.SMEM),
            pl.BlockSpec((rows, D_MODEL), lambda i: (i, 0)),
            pl.BlockSpec((MIX_TILE, D_MODEL), lambda i: (jnp.minimum(2 * i + 2, last_tile), 0)),
        ] + _mix_weight_specs(win, wgu, wbg, wbs, wout),
        out_specs=[pl.BlockSpec((rows, D_MODEL), lambda i: (i, 0)), state_spec, cache_spec, cache_spec],
        out_shape=_mix_out_shapes(batch, seq),
        scratch_shapes=_mix_scratch(WINDOW + rows)
        + [pltpu.VMEM((MIX_TILE, D_MODEL), BF16)] * 2
        + [pltpu.VMEM((MIX_TILE, w), dt) for w, dt in _PROJ_SHAPES] * 2,
        compiler_params=pltpu.CompilerParams(
            dimension_semantics=("arbitrary",), vmem_limit_bytes=VMEM_LIMIT),
    )(sinks, x, x, g, *win, wgu, bg, gn, wbg, wbs, wout)


def _mix_sample(x, batch, seq, sinks, g, win, wgu, bg, gn, wbg, wbs, wout, state0, ck0, cv0):
    rows = batch * seq
    whole = lambda a: pl.BlockSpec(a.shape, lambda i: (0,) * a.ndim)
    out_shape = _mix_out_shapes(batch, seq)
    return pl.pallas_call(
        functools.partial(_mix_sample_kernel, seq=seq),
        grid=(1,),
        in_specs=[pl.BlockSpec(memory_space=pltpu.SMEM), whole(x)]
        + _mix_weight_specs(win, wgu, wbg, wbs, wout) + [whole(state0), whole(ck0), whole(cv0)],
        out_specs=[whole(o) for o in out_shape],
        out_shape=out_shape,
        scratch_shapes=[pltpu.VMEM((batch, SWA_KEYS, LANES), F32), pltpu.VMEM((batch, SWA_KEYS, LANES), F32),
                        pltpu.VMEM((batch, GLA_HEADS, GLA_DK, GLA_DV), BF16), pltpu.VMEM((rows, D_MODEL), BF16)]
        + [pltpu.VMEM((rows, w), dt) for w, dt in _PROJ_SHAPES],
        compiler_params=pltpu.CompilerParams(
            dimension_semantics=("arbitrary",), vmem_limit_bytes=VMEM_LIMIT),
    )(sinks, x, g, *win, wgu, bg, gn, wbg, wbs, wout, state0, ck0, cv0)


BF16_ROWS = 16


def _cast_rows(rows, steps):
    tiles = rows // BF16_ROWS
    assert rows % BF16_ROWS == 0
    return BF16_ROWS * min(t for t in range(1, tiles + 1) if tiles % t == 0 and tiles // t <= steps)


def _layers(xp, xs, norm_ffn1, w_ffn1_in, w_ffn1_out, norm_mix, w_in, w_gate_up, b_gate, gla_norm, sinks,
            w_br_gla, w_br_swa, w_out, norm_ffn2, w_ffn2_in, w_ffn2_out, gf, state0, ck0, cv0):
    (bp, sp, _), (bs, ss, _) = xp.shape, xs.shape
    whole = lambda a: ((0, a.shape[1], a.shape[1]),)
    steps = bp * sp // FFN_ROWS + 1
    rb = lambda a: _cast_rows(a.shape[0], steps)
    rank0 = IN_HEAD_W
    casts = (
        (w_ffn2_in, rb(w_ffn2_in), whole(w_ffn2_in)),
        (w_ffn2_out, rb(w_ffn2_out), whole(w_ffn2_out)),
        (w_in, rb(w_in), ((0, rank0, rank0), (rank0, rank0 + GLA_RANK, RANK_PAD),
                           (rank0 + GLA_RANK, w_in.shape[1], IN_TAIL_W))),
        (w_br_gla, rb(w_br_gla), whole(w_br_gla)),
        (w_br_swa, rb(w_br_swa), whole(w_br_swa)),
        (w_out, rb(w_out), whole(w_out)),
    )
    x1p, x1s, w2a, w2b, head, rank, tail, wbg, wbs, wo = _ffn(
        xp.reshape(bp * sp, D_MODEL), xs.reshape(bs * ss, D_MODEL), norm_ffn1.reshape(1, D_MODEL),
        w_ffn1_in.astype(BF16), w_ffn1_out.astype(BF16), gf, final_norm=False, casts=casts)
    wgu = jnp.pad(w_gate_up, ((0, RANK_PAD - GLA_RANK), (0, 0))).astype(BF16)
    mixw = (sinks, norm_mix.reshape(1, D_MODEL), (head, rank, tail), wgu, b_gate.reshape(1, GLA_QK_W),
            gla_norm.reshape(1, GLA_DV), wbg, wbs, wo)
    x2p, *carry_p = _mix_prompt(x1p, bp, sp, *mixw)
    x2s, *carry_s = _mix_sample(x1s, bs, ss, *mixw, state0, ck0, cv0)
    yp, ys = _ffn(x2p, x2s, norm_ffn2.reshape(1, D_MODEL), w2a, w2b, gf, final_norm=True)

    def carry(batch, state, ck, cv):
        return (state[None], ck.reshape(1, batch, WINDOW, SWA_KV_HEADS, SWA_HD),
                cv.reshape(1, batch, WINDOW, SWA_KV_HEADS, SWA_HD))

    return (yp.reshape(bp, sp, D_MODEL), ys.reshape(bs, ss, D_MODEL), *carry(bp, *carry_p), *carry(bs, *carry_s))


def kernel(x_prompt, x_sample, state_gla, cache_swa_k, cache_swa_v, norm_ffn1, w_ffn1_in, w_ffn1_out, norm_mix, w_in, w_gla_gate_up, b_gla_gate, gla_norm, swa_sinks, w_branch_gla, w_branch_swa, w_out, norm_ffn2, w_ffn2_in, w_ffn2_out, norm_final):
    dec_batch = x_sample.shape[0]
    return _layers(x_prompt, x_sample, norm_ffn1[0], w_ffn1_in[0], w_ffn1_out[0], norm_mix[0], w_in[0],
                   w_gla_gate_up[0], b_gla_gate[0], gla_norm[0], swa_sinks[0], w_branch_gla[0], w_branch_swa[0],
                   w_out[0], norm_ffn2[0], w_ffn2_in[0], w_ffn2_out[0], norm_final.reshape(1, D_MODEL),
                   state_gla[0], cache_swa_k[0].reshape(dec_batch, WINDOW, SWA_KV_W),
                   cache_swa_v[0].reshape(dec_batch, WINDOW, SWA_KV_W))
```

```python
import functools

import jax
import jax.numpy as jnp
from jax import lax
from jax.experimental import pallas as pl
from jax.experimental.pallas import tpu as pltpu

F32 = jnp.float32
BF16 = jnp.bfloat16

D_MODEL = 1024
D_FF = 2816
CHUNK = 64
CHUNK_SHIFT = CHUNK.bit_length() - 1
EPS = 1e-6
LOG2E = 1.4426950408889634
GLA_HEADS = 4
GLA_DK = 128
GLA_DV = 256
GLA_RANK = 16
GLA_TAU = 16.0
SWA_HEADS = 16
SWA_KV_HEADS = 2
SWA_HD = 64
SWA_GROUP = SWA_HEADS // SWA_KV_HEADS
WINDOW = 128
GLA_QK_W = GLA_HEADS * GLA_DK
GLA_V_W = GLA_HEADS * GLA_DV
SWA_Q_W = SWA_HEADS * SWA_HD
SWA_KV_W = SWA_KV_HEADS * SWA_HD

LANES = 128
RANK_PAD = LANES
FF_CHUNK = 256
N_FF_CHUNKS = D_FF // FF_CHUNK
VMEM_LIMIT = 56 * 1024 * 1024

FFN_ROWS = 512
MIX_TILE = 256
GLA_BLOCK = 64
SWA_Q_ROWS = 2 * CHUNK
SWA_KEYS = 2 * WINDOW

IN_HEAD_W = 2 * GLA_QK_W + 2 * GLA_V_W
IN_TAIL_W = SWA_Q_W + 2 * SWA_KV_W + 2 * D_MODEL
_W_HEAD, _W_RANK, _W_TAIL = range(3)
_C_GQ = 0
_C_GK = _C_GQ + GLA_QK_W
_C_GLR = _C_GK + GLA_QK_W

_PROJ_LAYOUT = (
    (F32, None, ((_W_HEAD, 0, 2 * GLA_QK_W), (_W_RANK, 0, RANK_PAD))),
    (BF16, None, ((_W_HEAD, 2 * GLA_QK_W, GLA_V_W),)),
    (F32, None, ((_W_HEAD, 2 * GLA_QK_W + GLA_V_W, GLA_V_W),)),
    (BF16, LOG2E * SWA_HD ** -0.5, ((_W_TAIL, 0, SWA_Q_W),)),
    (F32, None, ((_W_TAIL, SWA_Q_W, 2 * SWA_KV_W),)),
    (F32, None, ((_W_TAIL, SWA_Q_W + 2 * SWA_KV_W, D_MODEL),)),
    (F32, None, ((_W_TAIL, SWA_Q_W + 2 * SWA_KV_W + D_MODEL, D_MODEL),)),
)
_PROJ_SHAPES = tuple((sum(w for _, _, w in segs), dt) for dt, _, segs in _PROJ_LAYOUT)
PROJ_TASK_COLS = 256


def _rmsnorm(x, g):
    ms = jnp.mean(x * x, axis=-1, keepdims=True)
    return x * lax.rsqrt(ms + EPS) * g


def _dot(a, b):
    return jnp.dot(a, b, preferred_element_type=F32)


def _dot_nt(a, b):
    return lax.dot_general(a, b, (((1,), (1,)), ((), ())), preferred_element_type=F32)


def _dot_tn(a, b):
    return lax.dot_general(a, b, (((0,), (0,)), ((), ())), preferred_element_type=F32)


def _ffn_rows(x_ref, o_ref, g_ref, w1_ref, w2_ref, gf_ref, h_ref, act_ref, final_norm, ahead=None):
    x = x_ref[...]
    n = x.shape[0]
    if ahead is None:
        h_ref[...] = _rmsnorm(x, g_ref[...]).astype(BF16)
    else:
        xa_ref, ha_ref = ahead
        per_chunk = -(-n // (N_FF_CHUNKS * 16)) * 16
    for c in range(N_FF_CHUNKS):
        cols = slice(c * FF_CHUNK, (c + 1) * FF_CHUNK)
        up_cols = slice(D_FF + c * FF_CHUNK, D_FF + (c + 1) * FF_CHUNK)
        h = h_ref[...]
        gate = _dot(h, w1_ref[:, cols])
        up = _dot(h, w1_ref[:, up_cols])
        act_ref[:, cols] = (gate * jax.nn.sigmoid(gate) * up).astype(BF16)
        if ahead is not None:
            rs = slice(min(n, c * per_chunk), min(n, (c + 1) * per_chunk))
            if rs.start < rs.stop:
                ha_ref[rs, :] = _rmsnorm(xa_ref[rs, :], g_ref[...]).astype(BF16)
    y = x + 0.5 * _dot(act_ref[...], w2_ref[...])
    if final_norm:
        y = _rmsnorm(y, gf_ref[...])
    o_ref[...] = y


def _cast_blocks(jobs, in_refs, out_refs):
    outs = iter(out_refs)
    for ranges, in_ref in zip(jobs, in_refs):
        for c0, c1, width in ranges:
            o_ref = next(outs)
            if c1 - c0 == width:
                o_ref[...] = in_ref[:, c0:c1].astype(BF16)
            else:
                tile = in_ref[:, c0:c0 + width]
                keep = lax.broadcasted_iota(jnp.int32, tile.shape, 1) < c1 - c0
                o_ref[...] = jnp.where(keep, tile, 0.0).astype(BF16)


def _ffn_kernel(*refs, final_norm, prompt_steps, cast_jobs):
    n_in = len(cast_jobs)
    n_out = sum(len(r) for r in cast_jobs)
    xp_ref, xa_ref, xs_ref, g_ref, w1_ref, w2_ref, gf_ref = refs[:7]
    cast_in = refs[7:7 + n_in]
    yp_ref, ys_ref = refs[7 + n_in:9 + n_in]
    cast_out = refs[9 + n_in:9 + n_in + n_out]
    h_ref, act_ref = refs[9 + n_in + n_out:]
    step = pl.program_id(0)
    rows_s = xs_ref.shape[0]
    slot = step % 2
    _cast_blocks(cast_jobs, cast_in, cast_out)

    @pl.when(step == 0)
    def _():
        h_ref[0] = _rmsnorm(xp_ref[...], g_ref[...]).astype(BF16)

    @pl.when(step < prompt_steps)
    def _():
        _ffn_rows(xp_ref, yp_ref, g_ref, w1_ref, w2_ref, gf_ref, h_ref.at[slot], act_ref, final_norm,
                  ahead=(xa_ref, h_ref.at[1 - slot]))

    @pl.when(step == prompt_steps)
    def _():
        _ffn_rows(xs_ref, ys_ref, g_ref, w1_ref, w2_ref, gf_ref,
                  h_ref.at[0, 0:rows_s], act_ref.at[0:rows_s], final_norm)


def _resident(shape):
    nd = len(shape)
    return pl.BlockSpec(shape, lambda *_: (0,) * nd, pipeline_mode=pl.Buffered(1))


def _ffn(xp, xs, g, w1, w2, gf, *, final_norm, casts=()):
    n_p, n_s = xp.shape[0], xs.shape[0]
    assert n_p % FFN_ROWS == 0 and n_s <= FFN_ROWS
    prompt_steps = n_p // FFN_ROWS
    prompt_spec = pl.BlockSpec((FFN_ROWS, D_MODEL), lambda i: (jnp.minimum(i, prompt_steps - 1), 0))
    sample_spec = pl.BlockSpec((n_s, D_MODEL), lambda i: (0, 0))
    cast_in_specs, cast_out_specs, cast_out_shapes = [], [], []
    for a, rb, ranges in casts:
        n_blocks = a.shape[0] // rb
        assert a.shape[0] % rb == 0 and n_blocks <= prompt_steps + 1
        index = functools.partial(lambda i, last: (jnp.minimum(i, last), 0), last=n_blocks - 1)
        cast_in_specs.append(pl.BlockSpec((rb, a.shape[1]), index))
        for _, _, width in ranges:
            cast_out_specs.append(pl.BlockSpec((rb, width), index))
            cast_out_shapes.append(jax.ShapeDtypeStruct((a.shape[0], width), BF16))
    return pl.pallas_call(
        functools.partial(_ffn_kernel, final_norm=final_norm, prompt_steps=prompt_steps,
                          cast_jobs=tuple(r for _, _, r in casts)),
        grid=(prompt_steps + 1,),
        in_specs=[
            prompt_spec,
            pl.BlockSpec((FFN_ROWS, D_MODEL), lambda i: (jnp.minimum(i + 1, prompt_steps - 1), 0)),
            sample_spec,
            _resident((1, D_MODEL)),
            _resident(w1.shape),
            _resident(w2.shape),
            _resident((1, D_MODEL)),
        ] + cast_in_specs,
        out_specs=[prompt_spec, sample_spec] + cast_out_specs,
        out_shape=[jax.ShapeDtypeStruct((n_p, D_MODEL), F32), jax.ShapeDtypeStruct((n_s, D_MODEL), F32)]
        + cast_out_shapes,
        scratch_shapes=[pltpu.VMEM((2, FFN_ROWS, D_MODEL), BF16), pltpu.VMEM((FFN_ROWS, D_FF), BF16)],
        compiler_params=pltpu.CompilerParams(
            dimension_semantics=("arbitrary",), vmem_limit_bytes=VMEM_LIMIT),
    )(xp, xp, xs, g, w1, w2, gf, *[a for a, _, _ in casts])


def _proj_tasks(h_ref, win_refs, p_refs):
    tasks = []
    for dst, (dtype, scale, segments) in zip(p_refs, _PROJ_LAYOUT):
        d0 = 0
        for part, col0, width in segments:
            for c in range(0, width, PROJ_TASK_COLS):
                w = min(PROJ_TASK_COLS, width - c)

                def task(dst=dst, d=d0 + c, w=w, src=win_refs[part], s=col0 + c, dtype=dtype, scale=scale):
                    v = _dot(h_ref[...], src[:, s:s + w])
                    if scale is not None:
                        v = v * scale
                    dst[:, d:d + w] = v.astype(dtype)

                tasks.append(task)
            d0 += width
    return tasks


class _Interleave:
    def __init__(self, tasks=()):
        self._tasks = list(tasks)

    def __call__(self, n=1):
        for _ in range(n):
            if self._tasks:
                self._tasks.pop(0)()

    def flush(self):
        self(len(self._tasks))


SUBLANES = 8


def _ref_rows(p, m, row):
    n_rows, width = p.shape
    if 4 * m <= SUBLANES:
        groups = p.reshape(n_rows // SUBLANES, SUBLANES, width)
        back = lambda k: pltpu.roll(groups, k % SUBLANES, 1).reshape(n_rows, width)
        if m == 1:
            return jnp.where((row & 1) == 0, p, back(1))
        r = row & 3
        return jnp.where(r == 0, back(-1), jnp.where(r == 1, p, jnp.where(r == 2, back(1), back(2))))
    nb = n_rows // (2 * m)
    p3 = p.reshape(nb, 2 * m, width)
    r3 = jnp.broadcast_to(p3[:, m - 1:m, :], (nb, 2 * m, width))
    return r3.reshape(n_rows, width)


def _gla_scan(q, k, la, fill):
    n_rows = q.shape[0]
    row = lax.broadcasted_iota(jnp.int32, (n_rows, 1), 0)
    p = la
    zs = []
    m = 1
    while m < n_rows:
        upper = (row & m) != 0
        r = _ref_rows(p, m, row)
        e = jnp.exp2(jnp.where(upper, p, r - p))
        zs.append((jnp.where(upper, q, k) * e).astype(BF16))
        p = p + jnp.where(upper, r, 0.0)
        fill()
        m *= 2
    b = p
    b_last = b[n_rows - 1:n_rows, :]
    return dict(zs=zs, qb=q.astype(BF16), kb=k.astype(BF16),
                q_in=(q * jnp.exp2(b)).astype(BF16), k_out=(k * jnp.exp2(b_last - b)).astype(BF16),
                s_decay=jnp.exp2(b_last))


def _gla_scores(sc, fill):
    n_rows = sc["qb"].shape[0]
    n_cols = max(n_rows, LANES)
    ti = lax.broadcasted_iota(jnp.int32, (n_rows, n_cols), 0)
    si = lax.broadcasted_iota(jnp.int32, (n_rows, n_cols), 1)
    split_level = 31 - lax.clz(jnp.where(ti > si, ti ^ si, 0))
    eye = ti == si
    pad = jnp.zeros((n_cols - n_rows, GLA_DK), BF16)

    def keys(x):
        return x if n_cols == n_rows else jnp.concatenate([x, pad], axis=0)

    a_heads = []
    for h in range(GLA_HEADS):
        ks = slice(h * GLA_DK, (h + 1) * GLA_DK)
        a = jnp.where(eye, _dot_nt(sc["qb"][:, ks], keys(sc["kb"][:, ks])), 0.0)
        for li, z in enumerate(sc["zs"]):
            a = jnp.where(split_level == li, _dot_nt(z[:, ks], keys(z[:, ks])), a)
        a_heads.append(a.astype(BF16))
        fill()
    return a_heads


def _gla_out(sc, a_heads, vb, s_ref, sb_ref, gnorm):
    n_rows = vb.shape[0]
    pad = jnp.zeros((a_heads[0].shape[1] - n_rows, GLA_DV), BF16)
    outs = []
    for h in range(GLA_HEADS):
        ks = slice(h * GLA_DK, (h + 1) * GLA_DK)
        vs = slice(h * GLA_DV, (h + 1) * GLA_DV)
        rhs = [vb[:, vs], sb_ref[h]] if pad.shape[0] == 0 else [vb[:, vs], pad, sb_ref[h]]
        o = _dot(jnp.concatenate([a_heads[h], sc["q_in"][:, ks]], axis=1), jnp.concatenate(rhs, axis=0))
        dcol = jnp.transpose(jnp.broadcast_to(sc["s_decay"][:, ks], (SUBLANES, GLA_DK)))[:, 0:1]
        s_new = dcol * s_ref[h] + _dot_tn(sc["k_out"][:, ks], vb[:, vs])
        s_ref[h] = s_new
        sb_ref[h] = s_new.astype(BF16)
        outs.append(_rmsnorm(o, gnorm))
    return jnp.concatenate(outs, axis=1)


def _gla_inputs(qkg, wgu_ref, bg_ref):
    logit = _dot(qkg[:, _C_GLR:_C_GLR + RANK_PAD].astype(BF16), wgu_ref[...]) + bg_ref[...]
    log_alpha = (jnp.minimum(logit, 0.0) - jnp.log1p(jnp.exp(-jnp.abs(logit)))) * (LOG2E / GLA_TAU)
    gq = qkg[:, _C_GQ:_C_GQ + GLA_QK_W] * (GLA_DK ** -0.5)
    gk = qkg[:, _C_GK:_C_GK + GLA_QK_W]
    return gq, gk, log_alpha


def _swa_operands(qb, kn, ks, vn, vs):
    nk = kn.shape[0]
    low = lax.broadcasted_iota(jnp.int32, (nk, LANES), 1) < SWA_HD
    pairs = SWA_GROUP // 2
    res = []
    for g in range(SWA_KV_HEADS):
        k_lo, k_hi = (kn, ks) if g == 0 else (ks, kn)
        v_lo, v_hi = (vn, vs) if g == 0 else (vs, vn)
        kbd = jnp.concatenate([jnp.where(low, k_lo, 0.0), jnp.where(low, 0.0, k_hi)], axis=0).astype(BF16)
        vbd = jnp.concatenate([jnp.where(low, v_lo, 0.0), jnp.where(low, 0.0, v_hi)], axis=0).astype(BF16)
        qs = jnp.concatenate(
            [qb[:, (g * pairs + i) * LANES:(g * pairs + i + 1) * LANES] for i in range(pairs)], axis=0)
        res.append((kbd, qs, vbd))
    return res


def _swa_scores(qb, kn, ks, vn, vs):
    return [(_dot_nt(qs, kbd), vbd) for kbd, qs, vbd in _swa_operands(qb, kn, ks, vn, vs)]


def _swa_softmax(scores, mask, sink_ref, fill):
    tq, nk = mask.shape
    low_o = lax.broadcasted_iota(jnp.int32, (tq, LANES), 1) < SWA_HD
    pairs = SWA_GROUP // 2
    res = []
    for g, (s, _) in enumerate(scores):
        p_rows, inv_rows = [], []
        for i in range(pairs):
            p_half, inv_half = [], []
            for half in range(2):
                sk = sink_ref[g * SWA_GROUP + 2 * i + half] * LOG2E
                sh = jnp.where(mask, s[i * tq:(i + 1) * tq, half * nk:(half + 1) * nk], -jnp.inf)
                mx = jnp.maximum(jnp.max(sh, axis=-1, keepdims=True), sk)
                ph = jnp.exp2(sh - mx)
                den = jnp.sum(ph, axis=-1, keepdims=True) + jnp.exp2(sk - mx)
                p_half.append(ph.astype(BF16))
                inv_half.append(1.0 / den)
            p_rows.append(jnp.concatenate(p_half, axis=1))
            inv_rows.append(jnp.where(low_o, inv_half[0], inv_half[1]))
            fill()
        res.append((jnp.concatenate(p_rows, axis=0), inv_rows))
    return res


def _swa_out(scores, probs):
    pairs = SWA_GROUP // 2
    tiles = []
    for (_, vbd), (p, inv_rows) in zip(scores, probs):
        tq = p.shape[0] // pairs
        o2 = _dot(p, vbd)
        tiles += [o2[i * tq:(i + 1) * tq, :] * inv_rows[i] for i in range(pairs)]
    return jnp.concatenate(tiles, axis=1).astype(BF16)


def _swa_scores_t(qb, kn, ks, vn, vs):
    return [(_dot_nt(kbd, qs), vbd) for kbd, qs, vbd in _swa_operands(qb, kn, ks, vn, vs)]


def _swa_softmax_t(scores_t, mask_t, sink_ref, fill):
    nk, tq = mask_t.shape
    low_o = lax.broadcasted_iota(jnp.int32, (tq, LANES), 1) < SWA_HD
    pairs = SWA_GROUP // 2
    res = []
    for g, (st, _) in enumerate(scores_t):
        p_cols, inv_tiles = [], []
        for i in range(pairs):
            p_half, inv_half = [], []
            for half in range(2):
                sk = sink_ref[g * SWA_GROUP + 2 * i + half] * LOG2E
                sh = jnp.where(mask_t, st[half * nk:(half + 1) * nk, i * tq:(i + 1) * tq], -jnp.inf)
                mx = jnp.maximum(jnp.max(sh, axis=0, keepdims=True), sk)
                ph = jnp.exp2(sh - mx)
                den = jnp.sum(ph, axis=0, keepdims=True) + jnp.exp2(sk - mx)
                p_half.append(ph.astype(BF16))
                inv_row = jnp.broadcast_to(1.0 / den, (SUBLANES, tq))
                inv_half.append(jnp.transpose(inv_row)[:, 0:1])
            p_cols.append(jnp.concatenate(p_half, axis=0))
            inv_tiles.append(jnp.where(low_o, inv_half[0], inv_half[1]))
            fill()
        res.append((jnp.concatenate(p_cols, axis=1), inv_tiles))
    return res


def _swa_out_t(scores_t, probs_t):
    pairs = SWA_GROUP // 2
    tiles = []
    for (_, vbd), (pt, inv_tiles) in zip(scores_t, probs_t):
        tq = pt.shape[1] // pairs
        o2 = _dot_tn(pt, vbd)
        tiles += [o2[i * tq:(i + 1) * tq, :] * inv_tiles[i] for i in range(pairs)]
    return jnp.concatenate(tiles, axis=1).astype(BF16)


def _swa_window(kx_ref, vx_ref, r0):
    kx = kx_ref[r0:r0 + SWA_KEYS, :]
    vx = vx_ref[r0:r0 + SWA_KEYS, :]
    return kx, pltpu.roll(kx, SWA_HD, 1), vx, pltpu.roll(vx, SWA_HD, 1)


def _prompt_mask_t(history_valid):
    kj = lax.broadcasted_iota(jnp.int32, (SWA_KEYS, SWA_Q_ROWS), 0)
    qi = lax.broadcasted_iota(jnp.int32, (SWA_KEYS, SWA_Q_ROWS), 1) >> CHUNK_SHIFT
    kc = kj >> CHUNK_SHIFT
    band = (kc >= qi) & (kc <= qi + WINDOW // CHUNK)
    if history_valid is None:
        return band
    return band & ((kj >= WINDOW) | history_valid)


def _merge(x, o_gla, o_swa, gg, gs, wbg_ref, wbs_ref, wout_ref):
    merged = jax.nn.sigmoid(gg) * _dot(o_gla, wbg_ref[...]) + jax.nn.sigmoid(gs) * _dot(o_swa, wbs_ref[...])
    return x + _dot(merged.astype(BF16), wout_ref[...])


def _mix_tile(p_tile, x, kx_ref, vx_ref, row0, history_valid, refs, fill):
    sink_ref, wgu_ref, bg_ref, gn_ref, wbg_ref, wbs_ref, wout_ref, s_ref, sb_ref = refs
    p_qkg, p_gv, p_gr, p_sq, _, p_gg, p_gs = p_tile
    gq, gk, log_alpha = _gla_inputs(p_qkg[...], wgu_ref, bg_ref)
    gn = gn_ref[...]
    groups = [slice(c * SWA_Q_ROWS, (c + 1) * SWA_Q_ROWS) for c in range(MIX_TILE // SWA_Q_ROWS)]
    per_group = SWA_Q_ROWS // GLA_BLOCK
    o_gla, o_swa = [], []
    for c, qs in enumerate(groups):
        blocks = [slice(qs.start + j * GLA_BLOCK, qs.start + (j + 1) * GLA_BLOCK) for j in range(per_group)]
        scs = [_gla_scan(gq[rs], gk[rs], log_alpha[rs], fill) for rs in blocks]
        scores = _swa_scores_t(p_sq[qs, :], *_swa_window(kx_ref, vx_ref, row0 + c * SWA_Q_ROWS))
        a_heads = [_gla_scores(sc, fill) for sc in scs]
        probs = _swa_softmax_t(scores, _prompt_mask_t(history_valid if c == 0 else None), sink_ref, fill)
        for rs, sc, ah in zip(blocks, scs, a_heads):
            o_gla.append(_gla_out(sc, ah, p_gv[rs, :], s_ref, sb_ref, gn))
        o_swa.append(_swa_out_t(scores, probs))
    fill.flush()
    gr = p_gr[...]
    og = (jnp.concatenate(o_gla, axis=0) * (gr * jax.nn.sigmoid(gr))).astype(BF16)
    return _merge(x, og, jnp.concatenate(o_swa, axis=0), p_gg[...], p_gs[...], wbg_ref, wbs_ref, wout_ref)


def _mix_prompt_kernel(sink_ref, x_ref, xn_ref, g_ref, wa_ref, wr_ref, wb_ref, wgu_ref, bg_ref, gn_ref, wbg_ref, wbs_ref,
                       wout_ref, y_ref, s_ref, ck_ref, cv_ref, kx_ref, vx_ref, sb_ref, h0_ref, h1_ref, *p_refs,
                       steps_per_seq):
    step = pl.program_id(0)
    seq_step = step % steps_per_seq
    t = MIX_TILE
    n_pieces = len(_PROJ_LAYOUT)
    p_tiles = (p_refs[:n_pieces], p_refs[n_pieces:])
    h_refs = (h0_ref, h1_ref)
    win_refs = (wa_ref, wr_ref, wb_ref)
    refs = (sink_ref, wgu_ref, bg_ref, gn_ref, wbg_ref, wbs_ref, wout_ref, s_ref, sb_ref)

    @pl.when(step == 0)
    def _():
        h0_ref[...] = _rmsnorm(x_ref[0:t, :], g_ref[...]).astype(BF16)
        _Interleave(_proj_tasks(h0_ref, win_refs, p_tiles[0])).flush()

    @pl.when(seq_step == 0)
    def _():
        s_ref[...] = jnp.zeros_like(s_ref)
        sb_ref[...] = jnp.zeros_like(sb_ref)
        ck_ref[...] = jnp.zeros_like(ck_ref)
        cv_ref[...] = jnp.zeros_like(cv_ref)

    kx_ref[0:WINDOW, :] = ck_ref[...]
    vx_ref[0:WINDOW, :] = cv_ref[...]
    for i in range(2):
        p_skv = p_tiles[i][4]
        x_ahead = x_ref[t:2 * t, :] if i == 0 else xn_ref[...]
        h_refs[1 - i][...] = _rmsnorm(x_ahead, g_ref[...]).astype(BF16)
        fill = _Interleave(_proj_tasks(h_refs[1 - i], win_refs, p_tiles[1 - i]))
        kx_ref[WINDOW + i * t:WINDOW + (i + 1) * t, :] = p_skv[:, :SWA_KV_W]
        vx_ref[WINDOW + i * t:WINDOW + (i + 1) * t, :] = p_skv[:, SWA_KV_W:]
        y_ref[i * t:(i + 1) * t, :] = _mix_tile(p_tiles[i], x_ref[i * t:(i + 1) * t, :], kx_ref, vx_ref, i * t,
                                                (seq_step > 0) if i == 0 else None, refs, fill)
    ck_ref[...] = kx_ref[2 * t:2 * t + WINDOW, :]
    cv_ref[...] = vx_ref[2 * t:2 * t + WINDOW, :]


def _mix_sample_kernel(sink_ref, x_ref, g_ref, wa_ref, wr_ref, wb_ref, wgu_ref, bg_ref, gn_ref, wbg_ref, wbs_ref, wout_ref,
                       s0_ref, ck0_ref, cv0_ref, y_ref, s_ref, ck_ref, cv_ref, kx_ref, vx_ref, sb_ref, h_ref,
                       *p_refs, seq):
    x = x_ref[...]
    batch = x.shape[0] // seq
    h_ref[...] = _rmsnorm(x, g_ref[...]).astype(BF16)
    _Interleave(_proj_tasks(h_ref, (wa_ref, wr_ref, wb_ref), p_refs)).flush()
    p_qkg, p_gv, p_gr, p_sq, p_skv, p_gg, p_gs = p_refs
    no_fill = _Interleave()
    gq, gk, log_alpha = _gla_inputs(p_qkg[...], wgu_ref, bg_ref)
    gn = gn_ref[...]
    visible = lax.broadcasted_iota(jnp.int32, (seq, SWA_KEYS), 1) < WINDOW + seq
    pad = jnp.zeros((SWA_KEYS - WINDOW - seq, LANES), F32)
    o_gla, o_swa = [], []
    for b in range(batch):
        rs = slice(b * seq, (b + 1) * seq)
        s_ref[b] = s0_ref[b]
        sb_ref[b] = s0_ref[b].astype(BF16)
        sc = _gla_scan(gq[rs], gk[rs], log_alpha[rs], no_fill)
        o_gla.append(_gla_out(sc, _gla_scores(sc, no_fill), p_gv[rs, :], s_ref.at[b], sb_ref.at[b], gn))

        kx_ref[b, 0:WINDOW, :] = ck0_ref[b]
        vx_ref[b, 0:WINDOW, :] = cv0_ref[b]
        kx_ref[b, WINDOW:WINDOW + seq, :] = p_skv[rs, :SWA_KV_W]
        vx_ref[b, WINDOW:WINDOW + seq, :] = p_skv[rs, SWA_KV_W:]
        kx_ref[b, WINDOW + seq:, :] = pad
        vx_ref[b, WINDOW + seq:, :] = pad
        ck_ref[b] = kx_ref[b, seq:seq + WINDOW, :]
        cv_ref[b] = vx_ref[b, seq:seq + WINDOW, :]
        scores = _swa_scores(p_sq[rs, :], *_swa_window(kx_ref.at[b], vx_ref.at[b], 0))
        o_swa.append(_swa_out(scores, _swa_softmax(scores, visible, sink_ref, no_fill)))
    gr = p_gr[...]
    og = (jnp.concatenate(o_gla, axis=0) * (gr * jax.nn.sigmoid(gr))).astype(BF16)
    y_ref[...] = _merge(x, og, jnp.concatenate(o_swa, axis=0), p_gg[...], p_gs[...], wbg_ref, wbs_ref, wout_ref)


def _mix_weight_specs(win, wgu, wbg, wbs, wout):
    return [_resident((1, D_MODEL))] + [_resident(w.shape) for w in win] + [
        _resident(wgu.shape),
        _resident((1, GLA_QK_W)),
        _resident((1, GLA_DV)),
        _resident(wbg.shape),
        _resident(wbs.shape),
        _resident(wout.shape),
    ]


def _mix_out_shapes(batch, seq):
    return [
        jax.ShapeDtypeStruct((batch * seq, D_MODEL), F32),
        jax.ShapeDtypeStruct((batch, GLA_HEADS, GLA_DK, GLA_DV), F32),
        jax.ShapeDtypeStruct((batch, WINDOW, SWA_KV_W), F32),
        jax.ShapeDtypeStruct((batch, WINDOW, SWA_KV_W), F32),
    ]


def _mix_scratch(key_rows):
    return [pltpu.VMEM((key_rows, LANES), F32), pltpu.VMEM((key_rows, LANES), F32),
            pltpu.VMEM((GLA_HEADS, GLA_DK, GLA_DV), BF16)]


def _mix_prompt(x, batch, seq, sinks, g, win, wgu, bg, gn, wbg, wbs, wout):
    rows = 2 * MIX_TILE
    steps_per_seq = seq // rows
    n_steps = batch * steps_per_seq
    last_tile = batch * seq // MIX_TILE - 1
    state_spec = pl.BlockSpec((None, GLA_HEADS, GLA_DK, GLA_DV), lambda i: (i // steps_per_seq, 0, 0, 0))
    cache_spec = pl.BlockSpec((None, WINDOW, SWA_KV_W), lambda i: (i // steps_per_seq, 0, 0))
    return pl.pallas_call(
        functools.partial(_mix_prompt_kernel, steps_per_seq=steps_per_seq),
        grid=(n_steps,),
        in_specs=[
            pl.BlockSpec(memory_space=pltpu.SMEM),
            pl.BlockSpec((rows, D_MODEL), lambda i: (i, 0)),
            pl.BlockSpec((MIX_TILE, D_MODEL), lambda i: (jnp.minimum(2 * i + 2, last_tile), 0)),
        ] + _mix_weight_specs(win, wgu, wbg, wbs, wout),
        out_specs=[pl.BlockSpec((rows, D_MODEL), lambda i: (i, 0)), state_spec, cache_spec, cache_spec],
        out_shape=_mix_out_shapes(batch, seq),
        scratch_shapes=_mix_scratch(WINDOW + rows)
        + [pltpu.VMEM((MIX_TILE, D_MODEL), BF16)] * 2
        + [pltpu.VMEM((MIX_TILE, w), dt) for w, dt in _PROJ_SHAPES] * 2,
        compiler_params=pltpu.CompilerParams(
            dimension_semantics=("arbitrary",), vmem_limit_bytes=VMEM_LIMIT),
    )(sinks, x, x, g, *win, wgu, bg, gn, wbg, wbs, wout)


def _mix_sample(x, batch, seq, sinks, g, win, wgu, bg, gn, wbg, wbs, wout, state0, ck0, cv0):
    rows = batch * seq
    whole = lambda a: pl.BlockSpec(a.shape, lambda i: (0,) * a.ndim)
    out_shape = _mix_out_shapes(batch, seq)
    return pl.pallas_call(
        functools.partial(_mix_sample_kernel, seq=seq),
        grid=(1,),
        in_specs=[pl.BlockSpec(memory_space=pltpu.SMEM), whole(x)]
        + _mix_weight_specs(win, wgu, wbg, wbs, wout) + [whole(state0), whole(ck0), whole(cv0)],
        out_specs=[whole(o) for o in out_shape],
        out_shape=out_shape,
        scratch_shapes=[pltpu.VMEM((batch, SWA_KEYS, LANES), F32), pltpu.VMEM((batch, SWA_KEYS, LANES), F32),
                        pltpu.VMEM((batch, GLA_HEADS, GLA_DK, GLA_DV), BF16), pltpu.VMEM((rows, D_MODEL), BF16)]
        + [pltpu.VMEM((rows, w), dt) for w, dt in _PROJ_SHAPES],
        compiler_params=pltpu.CompilerParams(
            dimension_semantics=("arbitrary",), vmem_limit_bytes=VMEM_LIMIT),
    )(sinks, x, g, *win, wgu, bg, gn, wbg, wbs, wout, state0, ck0, cv0)


BF16_ROWS = 16


def _cast_rows(rows, steps):
    tiles = rows // BF16_ROWS
    assert rows % BF16_ROWS == 0
    return BF16_ROWS * min(t for t in range(1, tiles + 1) if tiles % t == 0 and tiles // t <= steps)


def _layers(xp, xs, norm_ffn1, w_ffn1_in, w_ffn1_out, norm_mix, w_in, w_gate_up, b_gate, gla_norm, sinks,
            w_br_gla, w_br_swa, w_out, norm_ffn2, w_ffn2_in, w_ffn2_out, gf, state0, ck0, cv0):
    (bp, sp, _), (bs, ss, _) = xp.shape, xs.shape
    whole = lambda a: ((0, a.shape[1], a.shape[1]),)
    steps = bp * sp // FFN_ROWS + 1
    rb = lambda a: _cast_rows(a.shape[0], steps)
    rank0 = IN_HEAD_W
    casts = (
        (w_ffn2_in, rb(w_ffn2_in), whole(w_ffn2_in)),
        (w_ffn2_out, rb(w_ffn2_out), whole(w_ffn2_out)),
        (w_in, rb(w_in), ((0, rank0, rank0), (rank0, rank0 + GLA_RANK, RANK_PAD),
                           (rank0 + GLA_RANK, w_in.shape[1], IN_TAIL_W))),
        (w_br_gla, rb(w_br_gla), whole(w_br_gla)),
        (w_br_swa, rb(w_br_swa), whole(w_br_swa)),
        (w_out, rb(w_out), whole(w_out)),
    )
    x1p, x1s, w2a, w2b, head, rank, tail, wbg, wbs, wo = _ffn(
        xp.reshape(bp * sp, D_MODEL), xs.reshape(bs * ss, D_MODEL), norm_ffn1.reshape(1, D_MODEL),
        w_ffn1_in.astype(BF16), w_ffn1_out.astype(BF16), gf, final_norm=False, casts=casts)
    wgu = jnp.pad(w_gate_up, ((0, RANK_PAD - GLA_RANK), (0, 0))).astype(BF16)
    mixw = (sinks, norm_mix.reshape(1, D_MODEL), (head, rank, tail), wgu, b_gate.reshape(1, GLA_QK_W),
            gla_norm.reshape(1, GLA_DV), wbg, wbs, wo)
    x2p, *carry_p = _mix_prompt(x1p, bp, sp, *mixw)
    x2s, *carry_s = _mix_sample(x1s, bs, ss, *mixw, state0, ck0, cv0)
    yp, ys = _ffn(x2p, x2s, norm_ffn2.reshape(1, D_MODEL), w2a, w2b, gf, final_norm=True)

    def carry(batch, state, ck, cv):
        return (state[None], ck.reshape(1, batch, WINDOW, SWA_KV_HEADS, SWA_HD),
                cv.reshape(1, batch, WINDOW, SWA_KV_HEADS, SWA_HD))

    return (yp.reshape(bp, sp, D_MODEL), ys.reshape(bs, ss, D_MODEL), *carry(bp, *carry_p), *carry(bs, *carry_s))


def kernel(x_prompt, x_sample, state_gla, cache_swa_k, cache_swa_v, norm_ffn1, w_ffn1_in, w_ffn1_out, norm_mix, w_in, w_gla_gate_up, b_gla_gate, gla_norm, swa_sinks, w_branch_gla, w_branch_swa, w_out, norm_ffn2, w_ffn2_in, w_ffn2_out, norm_final):
    dec_batch = x_sample.shape[0]
    return _layers(x_prompt, x_sample, norm_ffn1[0], w_ffn1_in[0], w_ffn1_out[0], norm_mix[0], w_in[0],
                   w_gla_gate_up[0], b_gla_gate[0], gla_norm[0], swa_sinks[0], w_branch_gla[0], w_branch_swa[0],
                   w_out[0], norm_ffn2[0], w_ffn2_in[0], w_ffn2_out[0], norm_final.reshape(1, D_MODEL),
                   state_gla[0], cache_swa_k[0].reshape(dec_batch, WINDOW, SWA_KV_W),
                   cache_swa_v[0].reshape(dec_batch, WINDOW, SWA_KV_W))
```
